```python
import jax, jax.numpy as jnp
from jax import lax
import numpy as np

D_MODEL = 1024
BATCH = 2
SEQ = 8192
DEPTH = 1
DEC_BATCH = 128
DEC_SEQ = 8
PAST_LEN = 8192
PAGE_SIZE = 128

HEAD_DIM = 64
HEADS_PER_GROUP = 4
WINDOWS = (128, 512, 2048)
DILATIONS = (1, 4, 16)
N_GROUPS = len(WINDOWS)
N_ATTN_HEADS = N_GROUPS * HEADS_PER_GROUP
ATTN_WIDTH = N_ATTN_HEADS * HEAD_DIM
ATTN_OUT = HEADS_PER_GROUP * HEAD_DIM
ATTN_SCALE = HEAD_DIM ** -0.5
ROPE_THETA = 10000.0
CONV_WIDTH = 768
CONV_K = 3
IN_SPLITS = (ATTN_WIDTH, 2 * ATTN_WIDTH, 3 * ATTN_WIDTH,
             3 * ATTN_WIDTH + CONV_WIDTH, 3 * ATTN_WIDTH + 2 * CONV_WIDTH,
             3 * ATTN_WIDTH + 3 * CONV_WIDTH, 3 * ATTN_WIDTH + 3 * CONV_WIDTH + D_MODEL)
IN_WIDTH = 3 * ATTN_WIDTH + 3 * CONV_WIDTH + 2 * D_MODEL
N_KEYS = 128
N_EXPERTS = N_KEYS * N_KEYS
PEER_HEADS = 8
PEER_TOPK = 16
PEER_KEY_DIM = 256
PEER_BLOCK = 128
NORM_EPS = 1e-6

kernel_name = 'hybrid_dilated_swa_shortconv_peer_step'


def rmsnorm(x, g):
    xf = x.astype(jnp.float32)
    y = xf * lax.rsqrt(jnp.mean(xf * xf, axis=-1, keepdims=True) + NORM_EPS)
    return (y * g.astype(jnp.float32)).astype(x.dtype)


def rope(x, pos):
    half = HEAD_DIM // 2
    inv = ROPE_THETA ** (-jnp.arange(half, dtype=jnp.float32) / half)
    ang = pos.astype(jnp.float32)[:, None] * inv[None, :]
    cos = jnp.cos(ang)[None, :, None, :]
    sin = jnp.sin(ang)[None, :, None, :]
    xf = x.astype(jnp.float32)
    x1, x2 = xf[..., :half], xf[..., half:]
    return jnp.concatenate([x1 * cos - x2 * sin, x2 * cos + x1 * sin], axis=-1).astype(x.dtype)


def dilated_attn_prompt(q, k, v, dil, n_back):
    b, t, h, dh = q.shape
    n = t // dil
    blk = n_back
    nb = -(-n // blk)
    pad = nb * blk - n

    def split(a):
        a = a.reshape(b, n, dil, h, dh).transpose(0, 2, 1, 3, 4)
        a = jnp.pad(a, ((0, 0), (0, 0), (0, pad), (0, 0), (0, 0)))
        return a.reshape(b, dil, nb, blk, h, dh)

    def with_prev(a):
        prev = jnp.pad(a[:, :, :-1], ((0, 0), (0, 0), (1, 0), (0, 0), (0, 0), (0, 0)))
        return jnp.concatenate([prev, a], axis=3)

    qb = split(q)
    kb = with_prev(split(k))
    vb = with_prev(split(v))
    s = jnp.einsum('brnqhd,brnkhd->brnhqk', qb, kb, preferred_element_type=jnp.float32) * ATTN_SCALE
    qi = jnp.arange(blk)[:, None]
    ki = jnp.arange(2 * blk)[None, :]
    dist = qi + blk - ki
    key_row = (jnp.arange(nb)[:, None, None] - 1) * blk + ki[None]
    valid = (dist >= 0)[None] & (dist <= n_back)[None] & (key_row >= 0)
    s = jnp.where(valid[None, None, :, None], s, -jnp.inf)
    m = jnp.max(s, axis=-1, keepdims=True)
    p = jnp.exp(s - m)
    den = jnp.sum(p, axis=-1, keepdims=True)
    o = jnp.einsum('brnhqk,brnkhd->brnqhd', p, vb.astype(jnp.float32)) / jnp.swapaxes(den, 3, 4)
    lse = jnp.swapaxes((m + jnp.log(den))[..., 0], 3, 4)
    o = o.reshape(b, dil, nb * blk, h, dh)[:, :, :n].transpose(0, 2, 1, 3, 4).reshape(b, t, h, dh)
    lse = lse.reshape(b, dil, nb * blk, h)[:, :, :n].transpose(0, 2, 1, 3).reshape(b, t, h)
    return o, lse


def dilated_attn_sample(q, k_new, v_new, cache_kv, dil, n_back, past_len):
    l = cache_kv.shape[1]
    s_len = q.shape[1]
    new_kv = jnp.stack([k_new, v_new], axis=2).astype(cache_kv.dtype)
    full = jnp.concatenate([cache_kv, new_kv], axis=1)
    i = jnp.arange(s_len)[:, None]
    j = jnp.arange(n_back + 1)[None, :]
    back = i - j * dil
    valid = (past_len + back) >= 0
    g = full[:, jnp.clip(l + back, 0, None)]
    s = jnp.einsum('bshd,bskhd->bshk', q, g[:, :, :, 0], preferred_element_type=jnp.float32) * ATTN_SCALE
    s = jnp.where(valid[None, :, None, :], s, -jnp.inf)
    m = jnp.max(s, axis=-1, keepdims=True)
    p = jnp.exp(s - m)
    den = jnp.sum(p, axis=-1, keepdims=True)
    o = jnp.einsum('bshk,bskhd->bshd', p, g[:, :, :, 1].astype(jnp.float32)) / den
    lse = (m + jnp.log(den))[..., 0]
    return o, lse


def token_mix(h, pos, conv_prev, attend, w_in, conv_w, w_attn_o, w_conv_o, w_o):
    bsz, t, _ = h.shape
    z = h @ w_in
    q, k, v, bg, cg, hc, ga, gc = jnp.split(z, IN_SPLITS, axis=-1)
    q = rope(q.reshape(bsz, t, N_ATTN_HEADS, HEAD_DIM), pos)
    k = rope(k.reshape(bsz, t, N_ATTN_HEADS, HEAD_DIM), pos)
    v = v.reshape(bsz, t, N_ATTN_HEADS, HEAD_DIM)
    outs, lses = [], []
    for gi in range(N_GROUPS):
        sl = slice(gi * HEADS_PER_GROUP, (gi + 1) * HEADS_PER_GROUP)
        o, lse = attend(gi, q[:, :, sl], k[:, :, sl], v[:, :, sl])
        outs.append(o)
        lses.append(lse)
    wts = jax.nn.softmax(jnp.stack(lses), axis=0)
    o_attn = jnp.einsum('gbth,gbthd->bthd', wts, jnp.stack(outs)).reshape(bsz, t, ATTN_OUT).astype(h.dtype)
    u = cg * hc
    upad = jnp.concatenate([conv_prev.astype(u.dtype), u], axis=1)
    yc = conv_w[0] * upad[:, 0:t]
    for j in range(1, CONV_K):
        yc = yc + conv_w[j] * upad[:, j:j + t]
    zc = bg * yc
    mix = jax.nn.sigmoid(ga) * (o_attn @ w_attn_o) + jax.nn.sigmoid(gc) * (zc @ w_conv_o)
    return mix @ w_o, k, v, upad[:, t:]


def peer(x, w_query, sub_keys, expert_u, expert_v):
    bsz, t, d = x.shape
    n = bsz * t
    nc = -(-n // PEER_BLOCK)
    xf = jnp.pad(x.reshape(n, d), ((0, nc * PEER_BLOCK - n), (0, 0))).reshape(nc, PEER_BLOCK, d)
    half = PEER_KEY_DIM // 2

    def block(xb):
        q = (xb @ w_query).reshape(PEER_BLOCK, PEER_HEADS, 2, half)
        s1 = jnp.einsum('thd,kd->thk', q[:, :, 0], sub_keys[0], preferred_element_type=jnp.float32)
        s2 = jnp.einsum('thd,kd->thk', q[:, :, 1], sub_keys[1], preferred_element_type=jnp.float32)
        v1, i1 = lax.top_k(s1, PEER_TOPK)
        v2, i2 = lax.top_k(s2, PEER_TOPK)
        cand = (v1[..., :, None] + v2[..., None, :]).reshape(PEER_BLOCK, PEER_HEADS, PEER_TOPK * PEER_TOPK)
        cidx = (i1[..., :, None] * N_KEYS + i2[..., None, :]).reshape(PEER_BLOCK, PEER_HEADS, PEER_TOPK * PEER_TOPK)
        top_s, top_pos = lax.top_k(cand, PEER_TOPK)
        eidx = jnp.take_along_axis(cidx, top_pos, axis=-1)
        gate = jax.nn.softmax(top_s, axis=-1)
        act = jnp.einsum('td,thkd->thk', xb, expert_u[eidx], preferred_element_type=jnp.float32)
        coef = (gate * jax.nn.gelu(act)).astype(xb.dtype)
        return jnp.einsum('thk,thkd->td', coef, expert_v[eidx])

    out = lax.map(block, xf)
    return out.reshape(nc * PEER_BLOCK, d)[:n].reshape(bsz, t, d)


def decoder_layer(x, c, pos, conv_prev, attend, n1, n2, w_ada, b_ada, w_in, conv_w,
                  w_attn_o, w_conv_o, w_o, w_query, sub_keys, expert_u, expert_v):
    mod = jax.nn.silu(c) @ w_ada + b_ada
    sh1, sc1, g1, sh2, sc2, g2 = [m[:, None, :] for m in jnp.split(mod, 6, axis=-1)]
    h = rmsnorm(x, n1) * (1 + sc1) + sh1
    mo, k, v, conv_state = token_mix(h, pos, conv_prev, attend, w_in, conv_w, w_attn_o, w_conv_o, w_o)
    x = x + g1 * mo
    h = rmsnorm(x, n2) * (1 + sc2) + sh2
    x = x + g2 * peer(h, w_query, sub_keys, expert_u, expert_v)
    return x, k, v, conv_state


def setup_inputs(seed: int = 0) -> dict:
    key = jax.random.key(seed)
    ks = iter(jax.random.split(key, 32))

    def nrm(shape, scale):
        return jax.random.normal(next(ks), shape, jnp.float32) * scale

    cache_len = [min(w, PAST_LEN) for w in WINDOWS]
    return {
        'x_prompt': nrm((BATCH, SEQ, D_MODEL), 1.0),
        'x_sample': nrm((DEC_BATCH, DEC_SEQ, D_MODEL), 1.0),
        'cache_kv0': nrm((DEPTH, DEC_BATCH, cache_len[0], 2, HEADS_PER_GROUP, HEAD_DIM), 1.0),
        'cache_kv1': nrm((DEPTH, DEC_BATCH, cache_len[1], 2, HEADS_PER_GROUP, HEAD_DIM), 1.0),
        'cache_kv2': nrm((DEPTH, DEC_BATCH, cache_len[2], 2, HEADS_PER_GROUP, HEAD_DIM), 1.0),
        'state_conv': nrm((DEPTH, DEC_BATCH, CONV_K - 1, CONV_WIDTH), 1.0),
        'c_prompt': nrm((BATCH, D_MODEL), 1.0),
        'c_sample': nrm((DEC_BATCH, D_MODEL), 1.0),
        'norm1_g': 1.0 + nrm((DEPTH, D_MODEL), 0.1),
        'norm2_g': 1.0 + nrm((DEPTH, D_MODEL), 0.1),
        'norm_f_g': 1.0 + nrm((D_MODEL,), 0.1),
        'w_ada': nrm((DEPTH, D_MODEL, 6 * D_MODEL), 0.5 * D_MODEL ** -0.5),
        'b_ada': nrm((DEPTH, 6 * D_MODEL), 0.02),
        'w_in': nrm((DEPTH, D_MODEL, IN_WIDTH), D_MODEL ** -0.5),
        'conv_w': nrm((DEPTH, CONV_K, CONV_WIDTH), CONV_K ** -0.5),
        'w_attn_o': nrm((DEPTH, ATTN_OUT, D_MODEL), ATTN_OUT ** -0.5),
        'w_conv_o': nrm((DEPTH, CONV_WIDTH, D_MODEL), CONV_WIDTH ** -0.5),
        'w_o': nrm((DEPTH, D_MODEL, D_MODEL), D_MODEL ** -0.5),
        'w_query': nrm((DEPTH, D_MODEL, PEER_HEADS * PEER_KEY_DIM), D_MODEL ** -0.5),
        'sub_keys': nrm((DEPTH, 2, N_KEYS, PEER_KEY_DIM // 2), (PEER_KEY_DIM // 2) ** -0.5),
        'expert_u': nrm((DEPTH, N_EXPERTS, D_MODEL), D_MODEL ** -0.5),
        'expert_v': nrm((DEPTH, N_EXPERTS, D_MODEL), PEER_HEADS ** -0.5),
    }


def reference(x_prompt, x_sample, cache_kv0, cache_kv1, cache_kv2, state_conv, c_prompt, c_sample,
              norm1_g, norm2_g, norm_f_g, w_ada, b_ada, w_in, conv_w, w_attn_o, w_conv_o, w_o,
              w_query, sub_keys, expert_u, expert_v):
    caches = (cache_kv0, cache_kv1, cache_kv2)
    t_p = x_prompt.shape[1]
    pos_p = jnp.arange(t_p)
    pos_s = PAST_LEN + jnp.arange(x_sample.shape[1])
    yp, ys = x_prompt, x_sample
    kvp = [[] for _ in range(N_GROUPS)]
    kvs = [[] for _ in range(N_GROUPS)]
    convp, convs = [], []
    for l in range(DEPTH):
        wl = (norm1_g[l], norm2_g[l], w_ada[l], b_ada[l], w_in[l], conv_w[l], w_attn_o[l],
              w_conv_o[l], w_o[l], w_query[l], sub_keys[l], expert_u[l], expert_v[l])

        def attend_p(gi, q, k, v):
            return dilated_attn_prompt(q, k, v, DILATIONS[gi], WINDOWS[gi] // DILATIONS[gi])

        def attend_s(gi, q, k, v, l=l):
            return dilated_attn_sample(q, k, v, caches[gi][l], DILATIONS[gi],
                                       WINDOWS[gi] // DILATIONS[gi], PAST_LEN)

        conv0 = jnp.zeros((yp.shape[0], CONV_K - 1, CONV_WIDTH), yp.dtype)
        yp, kp, vp, cp = decoder_layer(yp, c_prompt, pos_p, conv0, attend_p, *wl)
        ys, ksn, vsn, cs = decoder_layer(ys, c_sample, pos_s, state_conv[l], attend_s, *wl)
        for gi in range(N_GROUPS):
            sl = slice(gi * HEADS_PER_GROUP, (gi + 1) * HEADS_PER_GROUP)
            lp = min(WINDOWS[gi], t_p)
            kvp[gi].append(jnp.stack([kp[:, t_p - lp:, sl], vp[:, t_p - lp:, sl]], axis=2))
            kvs[gi].append(jnp.stack([ksn[:, :, sl], vsn[:, :, sl]], axis=2))
        convp.append(cp)
        convs.append(cs)
    y_prompt = rmsnorm(yp, norm_f_g)
    y_sample = rmsnorm(ys, norm_f_g)
    return (y_prompt, y_sample,
            jnp.stack(kvp[0]), jnp.stack(kvp[1]), jnp.stack(kvp[2]), jnp.stack(convp),
            jnp.stack(kvs[0]), jnp.stack(kvs[1]), jnp.stack(kvs[2]), jnp.stack(convs))
```

```python
import functools

import jax
import jax.numpy as jnp
from jax import lax
from jax.experimental import pallas as pl
from jax.experimental.pallas import tpu as pltpu

F32 = jnp.float32
BF16 = jnp.bfloat16

D_MODEL = 1024
HEAD_DIM = 64
HEADS_PER_GROUP = 4
WINDOWS = (128, 512, 2048)
DILATIONS = (1, 4, 16)
N_GROUPS = 3
GROUP_WIDTH = HEADS_PER_GROUP * HEAD_DIM
ATTN_WIDTH = N_GROUPS * GROUP_WIDTH
CONV_WIDTH = 768
CONV_K = 3
IN_WIDTH = 3 * ATTN_WIDTH + 3 * CONV_WIDTH + 2 * D_MODEL
ATTN_SCALE = HEAD_DIM ** -0.5
ROPE_THETA = 10000.0
N_KEYS = 128
PEER_HEADS = 8
PEER_TOPK = 16
NORM_EPS = 1e-6
N_BACK = 128
PAST_LEN = 8192

LANES = 128
SUBLANES = 8
VMEM_LIMIT = 52 * 1024 * 1024

TM_PROJ = 256
TM_MERGE = 256
TM_FINAL = 256
TK_ROUTE = 256
TM_FFN = 512
EB_FFN = 1024
ATTN_BLK = 128


def _cparams(sem):
    return pltpu.CompilerParams(dimension_semantics=sem, vmem_limit_bytes=VMEM_LIMIT)


def _ada_kernel(c_ref, w_ref, b_ref, o_ref):
    c = c_ref[...]
    a = (c * jax.nn.sigmoid(c)).astype(BF16)
    o_ref[...] = jnp.dot(a, w_ref[...].astype(BF16), preferred_element_type=F32) + b_ref[...]


def _ada(c_all, w_ada, b_ada):
    rows = c_all.shape[0]
    n_out = w_ada.shape[1]
    tn = 1024
    return pl.pallas_call(
        _ada_kernel,
        out_shape=jax.ShapeDtypeStruct((rows, n_out), F32),
        grid=(n_out // tn,),
        in_specs=[pl.BlockSpec((rows, D_MODEL), lambda j: (0, 0)),
                  pl.BlockSpec((D_MODEL, tn), lambda j: (0, j)),
                  pl.BlockSpec((1, tn), lambda j: (0, j))],
        out_specs=pl.BlockSpec((rows, tn), lambda j: (0, j)),
        compiler_params=_cparams(("arbitrary",)),
        name="adaln",
    )(c_all, w_ada, b_ada.reshape(1, n_out))


def _rmsnorm_mod(x, g, scale, shift):
    ms = jnp.mean(x * x, axis=-1, keepdims=True)
    y = x * lax.rsqrt(ms + NORM_EPS) * g
    return y * (1.0 + scale) + shift


def _inproj_kernel(x_ref, sc_ref, sh_ref, n1_ref, cos_ref, sa_ref, sb_ref, w_ref,
                   q_ref, k_ref, v_ref, bg_ref, u_ref, ga_ref, gc_ref):
    h = _rmsnorm_mod(x_ref[...], n1_ref[...], sc_ref[...], sh_ref[...]).astype(BF16)

    def proj(lo, hi):
        return jnp.dot(h, w_ref[:, lo:hi], preferred_element_type=F32)

    cos, sa, sb = cos_ref[...], sa_ref[...], sb_ref[...]

    def rope_store(z, ref, mult):
        for c in range(ATTN_WIDTH // LANES):
            zc = z[:, c * LANES:(c + 1) * LANES]
            r = zc * cos + pltpu.roll(zc, LANES - 32, 1) * sa + pltpu.roll(zc, 32, 1) * sb
            if mult != 1.0:
                r = r * mult
            ref[:, c * LANES:(c + 1) * LANES] = r.astype(ref.dtype)

    o = 0
    rope_store(proj(o, o + ATTN_WIDTH), q_ref, ATTN_SCALE)
    o += ATTN_WIDTH
    rope_store(proj(o, o + ATTN_WIDTH), k_ref, 1.0)
    o += ATTN_WIDTH
    v_ref[...] = proj(o, o + ATTN_WIDTH)
    o += ATTN_WIDTH
    bg_ref[...] = proj(o, o + CONV_WIDTH)
    o += CONV_WIDTH
    cg = proj(o, o + CONV_WIDTH)
    o += CONV_WIDTH
    u_ref[...] = cg * proj(o, o + CONV_WIDTH)
    o += CONV_WIDTH
    ga_ref[...] = proj(o, o + D_MODEL)
    o += D_MODEL
    gc_ref[...] = proj(o, o + D_MODEL)


def _inproj(x2d, mod_specs, mods, tab_spec, tabs, n1, w_in_b):
    n = x2d.shape[0]
    tm = TM_PROJ
    row = lambda w: pl.BlockSpec((tm, w), lambda i: (i, 0))
    const = lambda s: pl.BlockSpec(s, lambda i: (0, 0))
    outs = [jax.ShapeDtypeStruct((n, ATTN_WIDTH), BF16),
            jax.ShapeDtypeStruct((n, ATTN_WIDTH), F32),
            jax.ShapeDtypeStruct((n, ATTN_WIDTH), F32),
            jax.ShapeDtypeStruct((n, CONV_WIDTH), F32),
            jax.ShapeDtypeStruct((n, CONV_WIDTH), F32),
            jax.ShapeDtypeStruct((n, D_MODEL), F32),
            jax.ShapeDtypeStruct((n, D_MODEL), F32)]
    return pl.pallas_call(
        _inproj_kernel,
        out_shape=outs,
        grid=(n // tm,),
        in_specs=[row(D_MODEL), mod_specs, mod_specs, const((1, D_MODEL)),
                  tab_spec, tab_spec, tab_spec, const((D_MODEL, IN_WIDTH))],
        out_specs=[row(ATTN_WIDTH)] * 3 + [row(CONV_WIDTH)] * 2 + [row(D_MODEL)] * 2,
        compiler_params=_cparams(("arbitrary",)),
        name="inproj",
    )(x2d, mods[0], mods[1], n1, tabs[0], tabs[1], tabs[2], w_in_b)


def _pattn_kernel(q_ref, kp_ref, kc_ref, vp_ref, vc_ref, o_ref, l_ref):
    nb = pl.program_id(2)
    q = q_ref[...]
    k = jnp.concatenate([kp_ref[...], kc_ref[...]], axis=0).astype(BF16)
    v = jnp.concatenate([vp_ref[...], vc_ref[...]], axis=0).astype(BF16)
    qi = lax.broadcasted_iota(jnp.int32, (ATTN_BLK, 2 * ATTN_BLK), 0)
    ki = lax.broadcasted_iota(jnp.int32, (ATTN_BLK, 2 * ATTN_BLK), 1)
    dist = qi + ATTN_BLK - ki
    lo = jnp.where(nb > 0, 0, ATTN_BLK)
    valid = (dist >= 0) & (dist <= N_BACK) & (ki >= lo)
    for h in range(HEADS_PER_GROUP):
        sl = slice(h * HEAD_DIM, (h + 1) * HEAD_DIM)
        s = lax.dot_general(q[:, sl], k[:, sl], (((1,), (1,)), ((), ())), preferred_element_type=F32)
        s = jnp.where(valid, s, -jnp.inf)
        m = jnp.max(s, axis=-1, keepdims=True)
        p = jnp.exp(s - m)
        den = jnp.sum(p, axis=-1, keepdims=True)
        o = jnp.dot(p.astype(BF16), v[:, sl], preferred_element_type=F32) / den
        o_ref[:, sl] = o
        l_ref[:, sl] = jnp.broadcast_to(m + jnp.log(den), (ATTN_BLK, HEAD_DIM))


def _pattn(q, k, v, g, batch, seq):
    d = DILATIONS[g]
    n = seq // d
    nblk = n // ATTN_BLK
    qv = q.reshape(batch, n, d * ATTN_WIDTH)
    kv = k.reshape(batch, n, d * ATTN_WIDTH)
    vv = v.reshape(batch, n, d * ATTN_WIDTH)
    ngw = ATTN_WIDTH // GROUP_WIDTH
    cur = pl.BlockSpec((None, ATTN_BLK, GROUP_WIDTH), lambda b, r, nb: (b, nb, r * ngw + g))
    prev = pl.BlockSpec((None, ATTN_BLK, GROUP_WIDTH),
                        lambda b, r, nb: (b, jnp.maximum(nb - 1, 0), r * ngw + g))
    out = pl.BlockSpec((None, ATTN_BLK, GROUP_WIDTH), lambda b, r, nb: (b, nb, r))
    o, l = pl.pallas_call(
        _pattn_kernel,
        out_shape=[jax.ShapeDtypeStruct((batch, n, d * GROUP_WIDTH), F32)] * 2,
        grid=(batch, d, nblk),
        in_specs=[cur, prev, cur, prev, cur],
        out_specs=[out, out],
        compiler_params=_cparams(("arbitrary",) * 3),
        name=f"pattn{g}",
    )(qv, kv, kv, vv, vv)
    return o.reshape(batch * seq, GROUP_WIDTH), l.reshape(batch * seq, GROUP_WIDTH)


def _sattn_kernel(q_ref, k_ref, v_ref, c0_ref, c1_ref, c2_ref, o_ref, *, s_len):
    q = q_ref[...].astype(F32)
    kn = k_ref[...]
    vn = v_ref[...]
    rows = HEADS_PER_GROUP * s_len
    lane_head = lax.broadcasted_iota(jnp.int32, (s_len, GROUP_WIDTH), 1) // HEAD_DIM
    pad = jnp.zeros((LANES - s_len, GROUP_WIDTH), F32)
    ms, ls, os_ = [], [], []
    for g, cref in enumerate((c0_ref, c1_ref, c2_ref)):
        d = DILATIONS[g]
        gs = slice(g * GROUP_WIDTH, (g + 1) * GROUP_WIDTH)
        qg = q[:, gs]
        qexp = jnp.concatenate(
            [jnp.where(lane_head == h, qg, 0.0) for h in range(HEADS_PER_GROUP)], axis=0).astype(BF16)
        blk = cref[...]
        if blk.ndim == 3:
            blk = blk.reshape(blk.shape[0] * blk.shape[1], blk.shape[2])
        r_cache = blk.shape[0]
        kall = jnp.concatenate([blk[:, :GROUP_WIDTH], kn[:, gs], pad], axis=0).astype(BF16)
        vall = jnp.concatenate([blk[:, GROUP_WIDTH:], vn[:, gs], pad], axis=0).astype(BF16)
        r_all = r_cache + LANES
        s = lax.dot_general(qexp, kall, (((1,), (1,)), ((), ())), preferred_element_type=F32)
        col = lax.broadcasted_iota(jnp.int32, (rows, r_all), 1)
        iq = lax.broadcasted_iota(jnp.int32, (rows, r_all), 0) % s_len
        if g == 2:
            cpos = (col // SUBLANES) * d + (col % SUBLANES)
        else:
            cpos = col
        back = jnp.where(col < r_cache, WINDOWS[g] + iq - cpos, iq - (col - r_cache))
        valid = (back >= 0) & (back <= WINDOWS[g]) & ((back & (d - 1)) == 0) & (col < r_cache + s_len)
        s = jnp.where(valid, s, -jnp.inf)
        m = jnp.max(s, axis=-1, keepdims=True)
        p = jnp.exp(s - m)
        ls.append(jnp.sum(p, axis=-1, keepdims=True))
        ms.append(m)
        os_.append(jnp.dot(p.astype(BF16), vall, preferred_element_type=F32))
    mm = jnp.maximum(jnp.maximum(ms[0], ms[1]), ms[2])
    num = jnp.zeros((rows, GROUP_WIDTH), F32)
    den = jnp.zeros((rows, 1), F32)
    for g in range(N_GROUPS):
        w = jnp.exp(ms[g] - mm)
        num = num + w * os_[g]
        den = den + w * ls[g]
    full = num / den
    out = jnp.zeros((s_len, GROUP_WIDTH), F32)
    for h in range(HEADS_PER_GROUP):
        out = out + jnp.where(lane_head == h, full[h * s_len:(h + 1) * s_len], 0.0)
    o_ref[...] = out


def _sattn(q, k, v, caches, n_seq, s_len):
    q3 = q.reshape(n_seq, s_len, ATTN_WIDTH)
    k3 = k.reshape(n_seq, s_len, ATTN_WIDTH)
    v3 = v.reshape(n_seq, s_len, ATTN_WIDTH)
    kvw = 2 * GROUP_WIDTH
    c0 = caches[0].reshape(n_seq, WINDOWS[0], kvw)
    c1 = caches[1].reshape(n_seq, WINDOWS[1], kvw)
    c2 = caches[2].reshape(n_seq, WINDOWS[2] // DILATIONS[2], DILATIONS[2], kvw)
    tok = pl.BlockSpec((None, s_len, ATTN_WIDTH), lambda b: (b, 0, 0))
    out = pl.pallas_call(
        functools.partial(_sattn_kernel, s_len=s_len),
        out_shape=jax.ShapeDtypeStruct((n_seq, s_len, GROUP_WIDTH), F32),
        grid=(n_seq,),
        in_specs=[tok, tok, tok,
                  pl.BlockSpec((None, WINDOWS[0], kvw), lambda b: (b, 0, 0)),
                  pl.BlockSpec((None, WINDOWS[1], kvw), lambda b: (b, 0, 0)),
                  pl.BlockSpec((None, WINDOWS[2] // DILATIONS[2], SUBLANES, kvw), lambda b: (b, 0, 0, 0))],
        out_specs=pl.BlockSpec((None, s_len, GROUP_WIDTH), lambda b: (b, 0, 0)),
        compiler_params=_cparams(("arbitrary",)),
        name="sattn",
    )(q3, k3, v3, c0, c1, c2)
    return out.reshape(n_seq * s_len, GROUP_WIDTH)


def _merge_tail(o_attn, bg, yc, ga_ref, gc_ref, x_ref, g1_ref, n2_ref, sc2_ref, sh2_ref,
                wa_ref, wc_ref, wo_ref, x1_ref, h2t_ref):
    a_out = jnp.dot(o_attn.astype(BF16), wa_ref[...], preferred_element_type=F32)
    c_out = jnp.dot((bg * yc).astype(BF16), wc_ref[...], preferred_element_type=F32)
    mix = jax.nn.sigmoid(ga_ref[...]) * a_out + jax.nn.sigmoid(gc_ref[...]) * c_out
    mo = jnp.dot(mix.astype(BF16), wo_ref[...], preferred_element_type=F32)
    x1 = x_ref[...] + g1_ref[...] * mo
    x1_ref[...] = x1
    h2 = _rmsnorm_mod(x1, n2_ref[...], sc2_ref[...], sh2_ref[...])
    h2t_ref[...] = h2.T.astype(BF16)


def _conv3(u, um1, um2, cw_ref):
    return cw_ref[0:1, :] * um2 + cw_ref[1:2, :] * um1 + cw_ref[2:3, :] * u


def _merge_prompt_kernel(o0_ref, o1_ref, o2_ref, l0_ref, l1_ref, l2_ref, bg_ref, u_ref, uh_ref, cw_ref,
                         ga_ref, gc_ref, x_ref, g1_ref, n2_ref, sc2_ref, sh2_ref, wa_ref, wc_ref, wo_ref,
                         x1_ref, h2t_ref, ubuf, *, blocks_per_seq):
    tm = u_ref.shape[0]
    first = (pl.program_id(0) % blocks_per_seq) == 0
    l0, l1, l2 = l0_ref[...], l1_ref[...], l2_ref[...]
    mm = jnp.maximum(jnp.maximum(l0, l1), l2)
    e0, e1, e2 = jnp.exp(l0 - mm), jnp.exp(l1 - mm), jnp.exp(l2 - mm)
    o_attn = (e0 * o0_ref[...] + e1 * o1_ref[...] + e2 * o2_ref[...]) / (e0 + e1 + e2)
    u = u_ref[...]
    ubuf[0:SUBLANES, :] = jnp.where(first, 0.0, uh_ref[...])
    ubuf[SUBLANES:SUBLANES + tm, :] = u
    yc = _conv3(u, ubuf[SUBLANES - 1:SUBLANES - 1 + tm, :], ubuf[SUBLANES - 2:SUBLANES - 2 + tm, :], cw_ref)
    _merge_tail(o_attn, bg_ref[...], yc, ga_ref, gc_ref, x_ref, g1_ref, n2_ref, sc2_ref, sh2_ref,
                wa_ref, wc_ref, wo_ref, x1_ref, h2t_ref)


def _merge_sample_kernel(oa_ref, bg_ref, u_ref, p1_ref, p2_ref, cw_ref,
                         ga_ref, gc_ref, x_ref, g1_ref, n2_ref, sc2_ref, sh2_ref, wa_ref, wc_ref, wo_ref,
                         x1_ref, h2t_ref, ubuf, *, s_len):
    tm = u_ref.shape[0]
    u = u_ref[...]
    ubuf[0:SUBLANES, :] = jnp.zeros((SUBLANES, CONV_WIDTH), F32)
    ubuf[SUBLANES:SUBLANES + tm, :] = u
    t = lax.broadcasted_iota(jnp.int32, (tm, CONV_WIDTH), 0) % s_len
    um1 = jnp.where(t < 1, p1_ref[...], ubuf[SUBLANES - 1:SUBLANES - 1 + tm, :])
    um2 = jnp.where(t < 2, p2_ref[...], ubuf[SUBLANES - 2:SUBLANES - 2 + tm, :])
    yc = _conv3(u, um1, um2, cw_ref)
    _merge_tail(oa_ref[...], bg_ref[...], yc, ga_ref, gc_ref, x_ref, g1_ref, n2_ref, sc2_ref, sh2_ref,
                wa_ref, wc_ref, wo_ref, x1_ref, h2t_ref)


def _merge_common_specs(tm, mod_spec):
    row = lambda w: pl.BlockSpec((tm, w), lambda i: (i, 0))
    const = lambda s: pl.BlockSpec(s, lambda i: (0, 0))
    ins = [row(D_MODEL), row(D_MODEL), row(D_MODEL), mod_spec, const((1, D_MODEL)), mod_spec, mod_spec,
           const((GROUP_WIDTH, D_MODEL)), const((CONV_WIDTH, D_MODEL)), const((D_MODEL, D_MODEL))]
    outs = [row(D_MODEL), pl.BlockSpec((D_MODEL, tm), lambda i: (0, i))]
    return ins, outs


def _merge_prompt(attn, bg, u, conv_w, ga, gc, x2d, mod_spec, g1, n2, sc2, sh2, wa, wc, wo, seq):
    n = x2d.shape[0]
    tm = TM_MERGE
    row = lambda w: pl.BlockSpec((tm, w), lambda i: (i, 0))
    halo = pl.BlockSpec((SUBLANES, CONV_WIDTH), lambda i: (jnp.maximum(i * (tm // SUBLANES) - 1, 0), 0))
    ins, outs = _merge_common_specs(tm, mod_spec)
    (o0, l0), (o1, l1), (o2, l2) = attn
    return pl.pallas_call(
        functools.partial(_merge_prompt_kernel, blocks_per_seq=seq // tm),
        out_shape=[jax.ShapeDtypeStruct((n, D_MODEL), F32), jax.ShapeDtypeStruct((D_MODEL, n), BF16)],
        grid=(n // tm,),
        in_specs=[row(GROUP_WIDTH)] * 6 + [row(CONV_WIDTH), row(CONV_WIDTH), halo,
                                           pl.BlockSpec((CONV_K, CONV_WIDTH), lambda i: (0, 0))] + ins,
        out_specs=outs,
        scratch_shapes=[pltpu.VMEM((SUBLANES + tm, CONV_WIDTH), F32)],
        compiler_params=_cparams(("arbitrary",)),
        name="merge_prompt",
    )(o0, o1, o2, l0, l1, l2, bg, u, u, conv_w, ga, gc, x2d, g1, n2, sc2, sh2, wa, wc, wo)


def _merge_sample(o_attn, bg, u, p1, p2, conv_w, ga, gc, x2d, mod_spec, g1, n2, sc2, sh2, wa, wc, wo, s_len):
    n = x2d.shape[0]
    tm = TM_MERGE
    row = lambda w: pl.BlockSpec((tm, w), lambda i: (i, 0))
    ins, outs = _merge_common_specs(tm, mod_spec)
    return pl.pallas_call(
        functools.partial(_merge_sample_kernel, s_len=s_len),
        out_shape=[jax.ShapeDtypeStruct((n, D_MODEL), F32), jax.ShapeDtypeStruct((D_MODEL, n), BF16)],
        grid=(n // tm,),
        in_specs=[row(GROUP_WIDTH)] + [row(CONV_WIDTH)] * 4 + [pl.BlockSpec((CONV_K, CONV_WIDTH), lambda i: (0, 0))] + ins,
        out_specs=outs,
        scratch_shapes=[pltpu.VMEM((SUBLANES + tm, CONV_WIDTH), F32)],
        compiler_params=_cparams(("arbitrary",)),
        name="merge_sample",
    )(o_attn, bg, u, p1, p2, conv_w, ga, gc, x2d, g1, n2, sc2, sh2, wa, wc, wo)


def _oddeven_merge_sort_pairs(n):
    pairs = []
    p = 1
    while p < n:
        k = p
        while k >= 1:
            for j in range(k % p, n - k, 2 * k):
                for i in range(min(k, n - j - k)):
                    if (i + j) // (2 * p) == (i + j + k) // (2 * p):
                        pairs.append((i + j, i + j + k))
            k //= 2
        p *= 2
    return pairs


_SORT16 = _oddeven_merge_sort_pairs(PEER_TOPK)


def _cmpx(x, a, b):
    hi, lo = jnp.maximum(x[a], x[b]), jnp.minimum(x[a], x[b])
    x[a], x[b] = hi, lo


def _bitonic_clean(x):
    stride = PEER_TOPK // 2
    while stride >= 1:
        for i in range(PEER_TOPK):
            if i & stride == 0:
                _cmpx(x, i, i + stride)
        stride //= 2
    return x


def _merge_top16(a, b):
    c = []
    for j in range(PEER_TOPK):
        jb = PEER_TOPK - 1 - j
        c.append(jnp.maximum(a[j], b[jb]) if jb < len(b) else a[j])
    return _bitonic_clean(c)


def _top16_over_keys(s):
    x = [s[j * SUBLANES:(j + 1) * SUBLANES, :] for j in range(N_KEYS // SUBLANES)]
    for a, b in _SORT16:
        _cmpx(x, a, b)
    shift = SUBLANES // 2
    while shift >= 1:
        x = _merge_top16(x, [pltpu.roll(v, shift, 0) for v in x])
        shift //= 2
    return x


def _route_kernel(h_ref, wq_ref, k1_ref, k2_ref, s1_ref, m1_ref, s2_ref, e2_ref, tau_ref):
    tk = h_ref.shape[1]
    qt = jnp.dot(wq_ref[...], h_ref[...], preferred_element_type=F32).astype(BF16)
    k1, k2 = k1_ref[...], k2_ref[...]
    sub = lax.broadcasted_iota(jnp.int32, (SUBLANES, tk), 0)
    half = N_KEYS
    s1s, s2s = [], []
    v1 = [jnp.zeros((SUBLANES, tk), F32) for _ in range(PEER_TOPK)]
    v2 = [jnp.zeros((SUBLANES, tk), F32) for _ in range(PEER_TOPK)]
    for h in range(PEER_HEADS):
        base = h * 2 * half
        s1 = jnp.dot(k1, qt[base:base + half], preferred_element_type=F32)
        s2 = jnp.dot(k2, qt[base + half:base + 2 * half], preferred_element_type=F32)
        s1s.append(s1)
        s2s.append(s2)
        t1 = _top16_over_keys(s1)
        t2 = _top16_over_keys(s2)
        for j in range(PEER_TOPK):
            v1[j] = jnp.where(sub == h, t1[j], v1[j])
            v2[j] = jnp.where(sub == h, t2[j], v2[j])
    lists = []
    for a in range(PEER_TOPK // 2):
        lists.append([v1[a] + v2[b] for b in range(PEER_TOPK // (a + 1))])
    lists.append([v1[a] + v2[0] for a in range(PEER_TOPK // 2, PEER_TOPK)])
    top = lists[0]
    for other in lists[1:]:
        top = _merge_top16(top, other)
    tau = top[PEER_TOPK - 1]
    z = jnp.ones((SUBLANES, tk), F32)
    for j in range(1, PEER_TOPK):
        z = z + jnp.exp(top[j] - top[0])
    rz = 1.0 / z
    tau_ref[...] = tau
    for h in range(PEER_HEADS):
        e1 = jnp.exp(s1s[h] - v1[0][h:h + 1, :]) * rz[h:h + 1, :]
        s1_ref[:, h, :, :] = s1s[h].reshape(N_KEYS // SUBLANES, SUBLANES, tk)
        m1_ref[:, h, :, :] = e1.reshape(N_KEYS // SUBLANES, SUBLANES, tk)
        s2_ref[h * N_KEYS:(h + 1) * N_KEYS, :] = s2s[h]
        e2_ref[h * N_KEYS:(h + 1) * N_KEYS, :] = jnp.exp(s2s[h] - v2[0][h:h + 1, :])


def _route(h2t, wq_t, k1, k2):
    n = h2t.shape[1]
    tk = TK_ROUTE
    a8 = N_KEYS // SUBLANES
    s14 = jax.ShapeDtypeStruct((a8, PEER_HEADS, SUBLANES, n), F32)
    s2d = jax.ShapeDtypeStruct((PEER_HEADS * N_KEYS, n), F32)
    spec4 = pl.BlockSpec((a8, PEER_HEADS, SUBLANES, tk), lambda i: (0, 0, 0, i))
    spec2 = pl.BlockSpec((PEER_HEADS * N_KEYS, tk), lambda i: (0, i))
    return pl.pallas_call(
        _route_kernel,
        out_shape=[s14, s14, s2d, s2d, jax.ShapeDtypeStruct((PEER_HEADS, n), F32)],
        grid=(n // tk,),
        in_specs=[pl.BlockSpec((D_MODEL, tk), lambda i: (0, i)),
                  pl.BlockSpec(wq_t.shape, lambda i: (0, 0)),
                  pl.BlockSpec(k1.shape, lambda i: (0, 0)),
                  pl.BlockSpec(k2.shape, lambda i: (0, 0))],
        out_specs=[spec4, spec4, spec2, spec2, pl.BlockSpec((PEER_HEADS, tk), lambda i: (0, i))],
        compiler_params=_cparams(("arbitrary",)),
        name="peer_route",
    )(h2t, wq_t, k1, k2)


FFN_LANE_TILE = 256
FFN_ROW_TILE = 64


def _ffn_kernel(h_ref, u_ref, vt_ref, s1_ref, m1_ref, s2_ref, e2_ref, tau_ref, o_ref, gelu_ref, coef_ref):
    tm = h_ref.shape[1]

    @pl.when(pl.program_id(1) == 0)
    def _():
        o_ref[...] = jnp.zeros_like(o_ref)

    act = jnp.dot(u_ref[...], h_ref[...], preferred_element_type=F32)
    gelu_ref[...] = jax.nn.gelu(act)
    for lt in range(tm // FFN_LANE_TILE):
        ls = slice(lt * FFN_LANE_TILE, (lt + 1) * FFN_LANE_TILE)
        for al in range(EB_FFN // N_KEYS):
            for rt in range(N_KEYS // FFN_ROW_TILE):
                acc = jnp.zeros((FFN_ROW_TILE, FFN_LANE_TILE), F32)
                for h in range(PEER_HEADS):
                    rs = slice(h * N_KEYS + rt * FFN_ROW_TILE, h * N_KEYS + (rt + 1) * FFN_ROW_TILE)
                    pair = s1_ref[h, al:al + 1, ls] + s2_ref[rs, ls]
                    acc = acc + jnp.where(pair >= tau_ref[h:h + 1, ls], e2_ref[rs, ls], 0.0) * m1_ref[h, al:al + 1, ls]
                es = slice(al * N_KEYS + rt * FFN_ROW_TILE, al * N_KEYS + (rt + 1) * FFN_ROW_TILE)
                coef_ref[es, ls] = (acc * gelu_ref[es, ls]).astype(BF16)
    o_ref[...] += jnp.dot(vt_ref[...], coef_ref[...], preferred_element_type=F32)


def _ffn(h2t, u_b, vt_b, s1, m1, s2, e2, tau):
    n = h2t.shape[1]
    tm, eb = TM_FFN, EB_FFN
    n_exp = u_b.shape[0]
    spec4 = pl.BlockSpec((None, PEER_HEADS, SUBLANES, tm), lambda i, e: (e, 0, 0, i))
    spec2 = pl.BlockSpec((PEER_HEADS * N_KEYS, tm), lambda i, e: (0, i))
    return pl.pallas_call(
        _ffn_kernel,
        out_shape=jax.ShapeDtypeStruct((D_MODEL, n), F32),
        grid=(n // tm, n_exp // eb),
        in_specs=[pl.BlockSpec((D_MODEL, tm), lambda i, e: (0, i)),
                  pl.BlockSpec((eb, D_MODEL), lambda i, e: (e, 0)),
                  pl.BlockSpec((D_MODEL, eb), lambda i, e: (0, e)),
                  spec4, spec4, spec2, spec2,
                  pl.BlockSpec((PEER_HEADS, tm), lambda i, e: (0, i))],
        out_specs=pl.BlockSpec((D_MODEL, tm), lambda i, e: (0, i)),
        scratch_shapes=[pltpu.VMEM((eb, tm), F32), pltpu.VMEM((eb, tm), BF16)],
        compiler_params=_cparams(("arbitrary", "arbitrary")),
        name="peer_ffn",
    )(h2t, u_b, vt_b, s1, m1, s2, e2, tau)


def _final_kernel(p_ref, x_ref, g2_ref, nf_ref, y_ref):
    x2 = x_ref[...] + g2_ref[...] * p_ref[...].T
    ms = jnp.mean(x2 * x2, axis=-1, keepdims=True)
    y_ref[...] = x2 * lax.rsqrt(ms + NORM_EPS) * nf_ref[...]


def _final(peer_t, col_off, x1, mod_spec, g2, nf):
    n = x1.shape[0]
    tm = TM_FINAL
    off = col_off // tm
    return pl.pallas_call(
        _final_kernel,
        out_shape=jax.ShapeDtypeStruct((n, D_MODEL), F32),
        grid=(n // tm,),
        in_specs=[pl.BlockSpec((D_MODEL, tm), lambda i: (0, i + off)),
                  pl.BlockSpec((tm, D_MODEL), lambda i: (i, 0)),
                  mod_spec,
                  pl.BlockSpec((1, D_MODEL), lambda i: (0, 0))],
        out_specs=pl.BlockSpec((tm, D_MODEL), lambda i: (i, 0)),
        compiler_params=_cparams(("arbitrary",)),
        name="final",
    )(peer_t, x1, g2, nf)


def _rope_tables(pos):
    half = HEAD_DIM // 2
    inv = ROPE_THETA ** (-jnp.arange(half, dtype=F32) / half)
    ang = pos.astype(F32)[:, None] * inv[None, :]
    cos, sin = jnp.cos(ang), jnp.sin(ang)
    zero = jnp.zeros_like(sin)
    reps = LANES // HEAD_DIM
    cos_t = jnp.tile(jnp.concatenate([cos, cos], axis=1), (1, reps))
    sa_t = jnp.tile(jnp.concatenate([-sin, zero], axis=1), (1, reps))
    sb_t = jnp.tile(jnp.concatenate([zero, sin], axis=1), (1, reps))
    return cos_t, sa_t, sb_t


def kernel(x_prompt, x_sample, cache_kv0, cache_kv1, cache_kv2, state_conv, c_prompt, c_sample,
           norm1_g, norm2_g, norm_f_g, w_ada, b_ada, w_in, conv_w, w_attn_o, w_conv_o, w_o,
           w_query, sub_keys, expert_u, expert_v):
    batch, seq, _ = x_prompt.shape
    n_seq, s_len, _ = x_sample.shape
    depth = w_in.shape[0]
    assert depth == 1
    assert tuple(c.shape[2] for c in (cache_kv0, cache_kv1, cache_kv2)) == WINDOWS
    assert s_len == SUBLANES and seq % (ATTN_BLK * DILATIONS[2]) == 0
    n_p, n_s = batch * seq, n_seq * s_len
    l = 0

    w_in_b = w_in[l].astype(BF16)
    wa_b, wc_b, wo_b = w_attn_o[l].astype(BF16), w_conv_o[l].astype(BF16), w_o[l].astype(BF16)
    wq_t = w_query[l].T.astype(BF16)
    k1_b, k2_b = sub_keys[l, 0].astype(BF16), sub_keys[l, 1].astype(BF16)
    u_b = expert_u[l].astype(BF16)
    vt_b = expert_v[l].T.astype(BF16)
    n1, n2, nf = norm1_g[l][None, :], norm2_g[l][None, :], norm_f_g[None, :]

    n_c = batch + n_seq
    c_all = jnp.concatenate([c_prompt, c_sample, jnp.zeros((-n_c % SUBLANES, D_MODEL), F32)], axis=0)
    mod = _ada(c_all, w_ada[l], b_ada[l])
    mod_p = [m[:, None, :] for m in jnp.split(mod[:batch], 6, axis=-1)]
    mod_s = jnp.split(jnp.repeat(mod[batch:n_c], s_len, axis=0), 6, axis=-1)

    def seq_spec(tm):
        return pl.BlockSpec((None, 1, D_MODEL), lambda i: (i // (seq // tm), 0, 0))

    def tok_spec(tm):
        return pl.BlockSpec((tm, D_MODEL), lambda i: (i, 0))

    tabs_p = _rope_tables(jnp.arange(seq))
    tabs_s = [jnp.tile(t, (n_seq, 1)) for t in _rope_tables(PAST_LEN + jnp.arange(s_len))]
    tab_p_spec = pl.BlockSpec((TM_PROJ, LANES), lambda i: (i % (seq // TM_PROJ), 0))
    tab_s_spec = pl.BlockSpec((TM_PROJ, LANES), lambda i: (i, 0))

    xp = x_prompt.reshape(n_p, D_MODEL)
    xs = x_sample.reshape(n_s, D_MODEL)
    qp, kp, vp, bgp, up, gap, gcp = _inproj(xp, seq_spec(TM_PROJ), (mod_p[1], mod_p[0]), tab_p_spec, tabs_p, n1, w_in_b)
    qs, ks, vs, bgs, us, gas, gcs = _inproj(xs, tok_spec(TM_PROJ), (mod_s[1], mod_s[0]), tab_s_spec, tabs_s, n1, w_in_b)

    attn_p = [_pattn(qp, kp, vp, g, batch, seq) for g in range(N_GROUPS)]
    caches = (cache_kv0[l], cache_kv1[l], cache_kv2[l])
    attn_s = _sattn(qs, ks, vs, caches, n_seq, s_len)

    x1p, h2tp = _merge_prompt(attn_p, bgp, up, conv_w[l], gap, gcp, xp, seq_spec(TM_MERGE),
                              mod_p[2], n2, mod_p[4], mod_p[3], wa_b, wc_b, wo_b, seq)
    st = state_conv[l]
    p1 = jnp.pad(st[:, 1:2], ((0, 0), (0, s_len - 1), (0, 0))).reshape(n_s, CONV_WIDTH)
    p2 = jnp.pad(st, ((0, 0), (0, s_len - 2), (0, 0))).reshape(n_s, CONV_WIDTH)
    x1s, h2ts = _merge_sample(attn_s, bgs, us, p1, p2, conv_w[l], gas, gcs, xs, tok_spec(TM_MERGE),
                              mod_s[2], n2, mod_s[4], mod_s[3], wa_b, wc_b, wo_b, s_len)

    h2t = jnp.concatenate([h2tp, h2ts], axis=1)
    s1, m1, s2, e2, tau = _route(h2t, wq_t, k1_b, k2_b)
    peer_t = _ffn(h2t, u_b, vt_b, s1, m1, s2, e2, tau)

    yp = _final(peer_t, 0, x1p, seq_spec(TM_FINAL), mod_p[5], nf).reshape(batch, seq, D_MODEL)
    ys = _final(peer_t, n_p, x1s, tok_spec(TM_FINAL), mod_s[5], nf).reshape(n_seq, s_len, D_MODEL)

    def kv_tail(k2d, v2d, g, rows, n_rows):
        gs = slice(g * GROUP_WIDTH, (g + 1) * GROUP_WIDTH)
        k4 = k2d.reshape(rows, n_rows, ATTN_WIDTH)[:, :, gs].reshape(rows, n_rows, HEADS_PER_GROUP, HEAD_DIM)
        v4 = v2d.reshape(rows, n_rows, ATTN_WIDTH)[:, :, gs].reshape(rows, n_rows, HEADS_PER_GROUP, HEAD_DIM)
        return k4, v4

    kv_p, kv_s = [], []
    for g in range(N_GROUPS):
        lp = min(WINDOWS[g], seq)
        k4, v4 = kv_tail(kp, vp, g, batch, seq)
        kv_p.append(jnp.stack([k4[:, seq - lp:], v4[:, seq - lp:]], axis=2)[None])
        k4, v4 = kv_tail(ks, vs, g, n_seq, s_len)
        kv_s.append(jnp.stack([k4, v4], axis=2)[None])
    conv_p = up.reshape(batch, seq, CONV_WIDTH)[:, seq - (CONV_K - 1):][None]
    conv_s = jnp.concatenate([st, us.reshape(n_seq, s_len, CONV_WIDTH)], axis=1)[:, s_len:][None]
    return (yp, ys, kv_p[0], kv_p[1], kv_p[2], conv_p, kv_s[0], kv_s[1], kv_s[2], conv_s)
```

```python
import functools

import jax
import jax.numpy as jnp
from jax import lax
from jax.experimental import pallas as pl
from jax.experimental.pallas import tpu as pltpu

F32 = jnp.float32
BF16 = jnp.bfloat16
U32 = jnp.uint32

D_MODEL = 1024
HEAD_DIM = 64
HEADS_PER_GROUP = 4
WINDOWS = (128, 512, 2048)
DILATIONS = (1, 4, 16)
N_GROUPS = 3
GROUP_WIDTH = HEADS_PER_GROUP * HEAD_DIM
ATTN_WIDTH = N_GROUPS * GROUP_WIDTH
CONV_WIDTH = 768
CONV_K = 3
IN_WIDTH = 3 * ATTN_WIDTH + 3 * CONV_WIDTH + 2 * D_MODEL
ATTN_SCALE = HEAD_DIM ** -0.5
ROPE_THETA = 10000.0
N_KEYS = 128
PEER_HEADS = 8
PEER_TOPK = 16
NORM_EPS = 1e-6
N_BACK = 128
PAST_LEN = 8192

LANES = 128
SUBLANES = 8
VMEM_LIMIT = 52 * 1024 * 1024

TM_PROJ = 256
TM_MERGE = 256
TM_FINAL = 256
TK_ROUTE = 256
TM_FFN = 1024
EB_FFN = 1024
ATTN_BLK = 128


def _pack_rows(x):
    m2, n = x.shape
    return lax.bitcast_convert_type(jnp.swapaxes(x.reshape(m2 // 2, 2, n), 1, 2), U32)


def _unpack(packed):
    return pltpu.bitcast(packed, BF16)


def _cparams(sem):
    return pltpu.CompilerParams(dimension_semantics=sem, vmem_limit_bytes=VMEM_LIMIT)


def _ada_kernel(c_ref, w_ref, b_ref, o_ref):
    c = c_ref[...]
    a = (c * jax.nn.sigmoid(c)).astype(BF16)
    o_ref[...] = jnp.dot(a, w_ref[...].astype(BF16), preferred_element_type=F32) + b_ref[...]


def _ada(c_all, w_ada, b_ada):
    rows = c_all.shape[0]
    n_out = w_ada.shape[1]
    tn = 1024
    return pl.pallas_call(
        _ada_kernel,
        out_shape=jax.ShapeDtypeStruct((rows, n_out), F32),
        grid=(n_out // tn,),
        in_specs=[pl.BlockSpec((rows, D_MODEL), lambda j: (0, 0)),
                  pl.BlockSpec((D_MODEL, tn), lambda j: (0, j)),
                  pl.BlockSpec((1, tn), lambda j: (0, j))],
        out_specs=pl.BlockSpec((rows, tn), lambda j: (0, j)),
        compiler_params=_cparams(("arbitrary",)),
        name="adaln",
    )(c_all, w_ada, b_ada.reshape(1, n_out))


def _rmsnorm_mod(x, g, scale, shift):
    ms = jnp.mean(x * x, axis=-1, keepdims=True)
    y = x * lax.rsqrt(ms + NORM_EPS) * g
    return y * (1.0 + scale) + shift


def _inproj_kernel(x_ref, sc_ref, sh_ref, n1_ref, cos_ref, sa_ref, sb_ref, w_ref,
                   q_ref, k_ref, v_ref, bg_ref, u_ref, ga_ref, gc_ref):
    h = _rmsnorm_mod(x_ref[...], n1_ref[...], sc_ref[...], sh_ref[...]).astype(BF16)

    def proj(lo, hi):
        return jnp.dot(h, w_ref[:, lo:hi], preferred_element_type=F32)

    cos, sa, sb = cos_ref[...], sa_ref[...], sb_ref[...]

    def rope_store(z, ref, mult):
        for c in range(ATTN_WIDTH // LANES):
            zc = z[:, c * LANES:(c + 1) * LANES]
            r = zc * cos + pltpu.roll(zc, LANES - 32, 1) * sa + pltpu.roll(zc, 32, 1) * sb
            if mult != 1.0:
                r = r * mult
            ref[:, c * LANES:(c + 1) * LANES] = r.astype(ref.dtype)

    o = 0
    rope_store(proj(o, o + ATTN_WIDTH), q_ref, ATTN_SCALE)
    o += ATTN_WIDTH
    rope_store(proj(o, o + ATTN_WIDTH), k_ref, 1.0)
    o += ATTN_WIDTH
    v_ref[...] = proj(o, o + ATTN_WIDTH)
    o += ATTN_WIDTH
    bg_ref[...] = proj(o, o + CONV_WIDTH)
    o += CONV_WIDTH
    cg = proj(o, o + CONV_WIDTH)
    o += CONV_WIDTH
    u_ref[...] = cg * proj(o, o + CONV_WIDTH)
    o += CONV_WIDTH
    ga_ref[...] = proj(o, o + D_MODEL)
    o += D_MODEL
    gc_ref[...] = proj(o, o + D_MODEL)


def _inproj(x2d, mod_specs, mods, tab_spec, tabs, n1, w_in_b):
    n = x2d.shape[0]
    tm = TM_PROJ
    row = lambda w: pl.BlockSpec((tm, w), lambda i: (i, 0))
    const = lambda s: pl.BlockSpec(s, lambda i: (0, 0))
    outs = [jax.ShapeDtypeStruct((n, ATTN_WIDTH), BF16),
            jax.ShapeDtypeStruct((n, ATTN_WIDTH), F32),
            jax.ShapeDtypeStruct((n, ATTN_WIDTH), F32),
            jax.ShapeDtypeStruct((n, CONV_WIDTH), F32),
            jax.ShapeDtypeStruct((n, CONV_WIDTH), F32),
            jax.ShapeDtypeStruct((n, D_MODEL), F32),
            jax.ShapeDtypeStruct((n, D_MODEL), F32)]
    return pl.pallas_call(
        _inproj_kernel,
        out_shape=outs,
        grid=(n // tm,),
        in_specs=[row(D_MODEL), mod_specs, mod_specs, const((1, D_MODEL)),
                  tab_spec, tab_spec, tab_spec, const((D_MODEL, IN_WIDTH))],
        out_specs=[row(ATTN_WIDTH)] * 3 + [row(CONV_WIDTH)] * 2 + [row(D_MODEL)] * 2,
        compiler_params=_cparams(("arbitrary",)),
        name="inproj",
    )(x2d, mods[0], mods[1], n1, tabs[0], tabs[1], tabs[2], w_in_b)


def _pattn_kernel(q_ref, kp_ref, kc_ref, vp_ref, vc_ref, o_ref, l_ref):
    nb = pl.program_id(2)
    q = q_ref[...]
    k = jnp.concatenate([kp_ref[...], kc_ref[...]], axis=0).astype(BF16)
    v = jnp.concatenate([vp_ref[...], vc_ref[...]], axis=0).astype(BF16)
    qi = lax.broadcasted_iota(jnp.int32, (ATTN_BLK, 2 * ATTN_BLK), 0)
    ki = lax.broadcasted_iota(jnp.int32, (ATTN_BLK, 2 * ATTN_BLK), 1)
    dist = qi + ATTN_BLK - ki
    lo = jnp.where(nb > 0, 0, ATTN_BLK)
    valid = (dist >= 0) & (dist <= N_BACK) & (ki >= lo)
    for h in range(HEADS_PER_GROUP):
        sl = slice(h * HEAD_DIM, (h + 1) * HEAD_DIM)
        s = lax.dot_general(q[:, sl], k[:, sl], (((1,), (1,)), ((), ())), preferred_element_type=F32)
        s = jnp.where(valid, s, -jnp.inf)
        m = jnp.max(s, axis=-1, keepdims=True)
        p = jnp.exp(s - m)
        den = jnp.sum(p, axis=-1, keepdims=True)
        o = jnp.dot(p.astype(BF16), v[:, sl], preferred_element_type=F32) / den
        o_ref[:, sl] = o
        l_ref[:, sl] = jnp.broadcast_to(m + jnp.log(den), (ATTN_BLK, HEAD_DIM))


def _pattn(q, k, v, g, batch, seq):
    d = DILATIONS[g]
    n = seq // d
    nblk = n // ATTN_BLK
    qv = q.reshape(batch, n, d * ATTN_WIDTH)
    kv = k.reshape(batch, n, d * ATTN_WIDTH)
    vv = v.reshape(batch, n, d * ATTN_WIDTH)
    ngw = ATTN_WIDTH // GROUP_WIDTH
    cur = pl.BlockSpec((None, ATTN_BLK, GROUP_WIDTH), lambda b, r, nb: (b, nb, r * ngw + g))
    prev = pl.BlockSpec((None, ATTN_BLK, GROUP_WIDTH),
                        lambda b, r, nb: (b, jnp.maximum(nb - 1, 0), r * ngw + g))
    out = pl.BlockSpec((None, ATTN_BLK, GROUP_WIDTH), lambda b, r, nb: (b, nb, r))
    o, l = pl.pallas_call(
        _pattn_kernel,
        out_shape=[jax.ShapeDtypeStruct((batch, n, d * GROUP_WIDTH), F32)] * 2,
        grid=(batch, d, nblk),
        in_specs=[cur, prev, cur, prev, cur],
        out_specs=[out, out],
        compiler_params=_cparams(("arbitrary",) * 3),
        name=f"pattn{g}",
    )(qv, kv, kv, vv, vv)
    return o.reshape(batch * seq, GROUP_WIDTH), l.reshape(batch * seq, GROUP_WIDTH)


def _sattn_kernel(q_ref, k_ref, v_ref, c0_ref, c1_ref, c2_ref, o_ref, *, s_len):
    q = q_ref[...].astype(F32)
    kn = k_ref[...]
    vn = v_ref[...]
    rows = HEADS_PER_GROUP * s_len
    lane_head = lax.broadcasted_iota(jnp.int32, (s_len, GROUP_WIDTH), 1) // HEAD_DIM
    pad = jnp.zeros((LANES - s_len, GROUP_WIDTH), F32)
    ms, ls, os_ = [], [], []
    for g, cref in enumerate((c0_ref, c1_ref, c2_ref)):
        d = DILATIONS[g]
        gs = slice(g * GROUP_WIDTH, (g + 1) * GROUP_WIDTH)
        qg = q[:, gs]
        qexp = jnp.concatenate(
            [jnp.where(lane_head == h, qg, 0.0) for h in range(HEADS_PER_GROUP)], axis=0).astype(BF16)
        blk = cref[...]
        if blk.ndim == 3:
            blk = blk.reshape(blk.shape[0] * blk.shape[1], blk.shape[2])
        r_cache = blk.shape[0]
        kall = jnp.concatenate([blk[:, :GROUP_WIDTH], kn[:, gs], pad], axis=0).astype(BF16)
        vall = jnp.concatenate([blk[:, GROUP_WIDTH:], vn[:, gs], pad], axis=0).astype(BF16)
        r_all = r_cache + LANES
        s = lax.dot_general(qexp, kall, (((1,), (1,)), ((), ())), preferred_element_type=F32)
        col = lax.broadcasted_iota(jnp.int32, (rows, r_all), 1)
        iq = lax.broadcasted_iota(jnp.int32, (rows, r_all), 0) % s_len
        if g == 2:
            cpos = (col // SUBLANES) * d + (col % SUBLANES)
        else:
            cpos = col
        back = jnp.where(col < r_cache, WINDOWS[g] + iq - cpos, iq - (col - r_cache))
        valid = (back >= 0) & (back <= WINDOWS[g]) & ((back & (d - 1)) == 0) & (col < r_cache + s_len)
        s = jnp.where(valid, s, -jnp.inf)
        m = jnp.max(s, axis=-1, keepdims=True)
        p = jnp.exp(s - m)
        ls.append(jnp.sum(p, axis=-1, keepdims=True))
        ms.append(m)
        os_.append(jnp.dot(p.astype(BF16), vall, preferred_element_type=F32))
    mm = jnp.maximum(jnp.maximum(ms[0], ms[1]), ms[2])
    num = jnp.zeros((rows, GROUP_WIDTH), F32)
    den = jnp.zeros((rows, 1), F32)
    for g in range(N_GROUPS):
        w = jnp.exp(ms[g] - mm)
        num = num + w * os_[g]
        den = den + w * ls[g]
    full = num / den
    out = jnp.zeros((s_len, GROUP_WIDTH), F32)
    for h in range(HEADS_PER_GROUP):
        out = out + jnp.where(lane_head == h, full[h * s_len:(h + 1) * s_len], 0.0)
    o_ref[...] = out


def _sattn(q, k, v, caches, n_seq, s_len):
    q3 = q.reshape(n_seq, s_len, ATTN_WIDTH)
    k3 = k.reshape(n_seq, s_len, ATTN_WIDTH)
    v3 = v.reshape(n_seq, s_len, ATTN_WIDTH)
    kvw = 2 * GROUP_WIDTH
    c0 = caches[0].reshape(n_seq, WINDOWS[0], kvw)
    c1 = caches[1].reshape(n_seq, WINDOWS[1], kvw)
    c2 = caches[2].reshape(n_seq, WINDOWS[2] // DILATIONS[2], DILATIONS[2], kvw)
    tok = pl.BlockSpec((None, s_len, ATTN_WIDTH), lambda b: (b, 0, 0))
    out = pl.pallas_call(
        functools.partial(_sattn_kernel, s_len=s_len),
        out_shape=jax.ShapeDtypeStruct((n_seq, s_len, GROUP_WIDTH), F32),
        grid=(n_seq,),
        in_specs=[tok, tok, tok,
                  pl.BlockSpec((None, WINDOWS[0], kvw), lambda b: (b, 0, 0)),
                  pl.BlockSpec((None, WINDOWS[1], kvw), lambda b: (b, 0, 0)),
                  pl.BlockSpec((None, WINDOWS[2] // DILATIONS[2], SUBLANES, kvw), lambda b: (b, 0, 0, 0))],
        out_specs=pl.BlockSpec((None, s_len, GROUP_WIDTH), lambda b: (b, 0, 0)),
        compiler_params=_cparams(("arbitrary",)),
        name="sattn",
    )(q3, k3, v3, c0, c1, c2)
    return out.reshape(n_seq * s_len, GROUP_WIDTH)


def _merge_tail(o_attn, bg, yc, ga_ref, gc_ref, x_ref, g1_ref, n2_ref, sc2_ref, sh2_ref,
                wa_ref, wc_ref, wo_ref, x1_ref, h2t_ref):
    a_out = jnp.dot(o_attn.astype(BF16), wa_ref[...], preferred_element_type=F32)
    c_out = jnp.dot((bg * yc).astype(BF16), wc_ref[...], preferred_element_type=F32)
    mix = jax.nn.sigmoid(ga_ref[...]) * a_out + jax.nn.sigmoid(gc_ref[...]) * c_out
    mo = jnp.dot(mix.astype(BF16), wo_ref[...], preferred_element_type=F32)
    x1 = x_ref[...] + g1_ref[...] * mo
    x1_ref[...] = x1
    h2 = _rmsnorm_mod(x1, n2_ref[...], sc2_ref[...], sh2_ref[...])
    h2t_ref[...] = pltpu.bitcast(h2.T.astype(BF16), U32)


def _conv3(u, um1, um2, cw_ref):
    return cw_ref[0:1, :] * um2 + cw_ref[1:2, :] * um1 + cw_ref[2:3, :] * u


def _merge_prompt_kernel(o0_ref, o1_ref, o2_ref, l0_ref, l1_ref, l2_ref, bg_ref, u_ref, uh_ref, cw_ref,
                         ga_ref, gc_ref, x_ref, g1_ref, n2_ref, sc2_ref, sh2_ref, wa_ref, wc_ref, wo_ref,
                         x1_ref, h2t_ref, ubuf, *, blocks_per_seq):
    tm = u_ref.shape[0]
    first = (pl.program_id(0) % blocks_per_seq) == 0
    l0, l1, l2 = l0_ref[...], l1_ref[...], l2_ref[...]
    mm = jnp.maximum(jnp.maximum(l0, l1), l2)
    e0, e1, e2 = jnp.exp(l0 - mm), jnp.exp(l1 - mm), jnp.exp(l2 - mm)
    o_attn = (e0 * o0_ref[...] + e1 * o1_ref[...] + e2 * o2_ref[...]) / (e0 + e1 + e2)
    u = u_ref[...]
    ubuf[0:SUBLANES, :] = jnp.where(first, 0.0, uh_ref[...])
    ubuf[SUBLANES:SUBLANES + tm, :] = u
    yc = _conv3(u, ubuf[SUBLANES - 1:SUBLANES - 1 + tm, :], ubuf[SUBLANES - 2:SUBLANES - 2 + tm, :], cw_ref)
    _merge_tail(o_attn, bg_ref[...], yc, ga_ref, gc_ref, x_ref, g1_ref, n2_ref, sc2_ref, sh2_ref,
                wa_ref, wc_ref, wo_ref, x1_ref, h2t_ref)


def _merge_sample_kernel(oa_ref, bg_ref, u_ref, p1_ref, p2_ref, cw_ref,
                         ga_ref, gc_ref, x_ref, g1_ref, n2_ref, sc2_ref, sh2_ref, wa_ref, wc_ref, wo_ref,
                         x1_ref, h2t_ref, ubuf, *, s_len):
    tm = u_ref.shape[0]
    u = u_ref[...]
    ubuf[0:SUBLANES, :] = jnp.zeros((SUBLANES, CONV_WIDTH), F32)
    ubuf[SUBLANES:SUBLANES + tm, :] = u
    t = lax.broadcasted_iota(jnp.int32, (tm, CONV_WIDTH), 0) % s_len
    um1 = jnp.where(t < 1, p1_ref[...], ubuf[SUBLANES - 1:SUBLANES - 1 + tm, :])
    um2 = jnp.where(t < 2, p2_ref[...], ubuf[SUBLANES - 2:SUBLANES - 2 + tm, :])
    yc = _conv3(u, um1, um2, cw_ref)
    _merge_tail(oa_ref[...], bg_ref[...], yc, ga_ref, gc_ref, x_ref, g1_ref, n2_ref, sc2_ref, sh2_ref,
                wa_ref, wc_ref, wo_ref, x1_ref, h2t_ref)


def _merge_common_specs(tm, mod_spec):
    row = lambda w: pl.BlockSpec((tm, w), lambda i: (i, 0))
    const = lambda s: pl.BlockSpec(s, lambda i: (0, 0))
    ins = [row(D_MODEL), row(D_MODEL), row(D_MODEL), mod_spec, const((1, D_MODEL)), mod_spec, mod_spec,
           const((GROUP_WIDTH, D_MODEL)), const((CONV_WIDTH, D_MODEL)), const((D_MODEL, D_MODEL))]
    outs = [row(D_MODEL), pl.BlockSpec((D_MODEL // 2, tm), lambda i: (0, i))]
    return ins, outs


def _merge_prompt(attn, bg, u, conv_w, ga, gc, x2d, mod_spec, g1, n2, sc2, sh2, wa, wc, wo, seq):
    n = x2d.shape[0]
    tm = TM_MERGE
    row = lambda w: pl.BlockSpec((tm, w), lambda i: (i, 0))
    halo = pl.BlockSpec((SUBLANES, CONV_WIDTH), lambda i: (jnp.maximum(i * (tm // SUBLANES) - 1, 0), 0))
    ins, outs = _merge_common_specs(tm, mod_spec)
    (o0, l0), (o1, l1), (o2, l2) = attn
    return pl.pallas_call(
        functools.partial(_merge_prompt_kernel, blocks_per_seq=seq // tm),
        out_shape=[jax.ShapeDtypeStruct((n, D_MODEL), F32), jax.ShapeDtypeStruct((D_MODEL // 2, n), U32)],
        grid=(n // tm,),
        in_specs=[row(GROUP_WIDTH)] * 6 + [row(CONV_WIDTH), row(CONV_WIDTH), halo,
                                           pl.BlockSpec((CONV_K, CONV_WIDTH), lambda i: (0, 0))] + ins,
        out_specs=outs,
        scratch_shapes=[pltpu.VMEM((SUBLANES + tm, CONV_WIDTH), F32)],
        compiler_params=_cparams(("arbitrary",)),
        name="merge_prompt",
    )(o0, o1, o2, l0, l1, l2, bg, u, u, conv_w, ga, gc, x2d, g1, n2, sc2, sh2, wa, wc, wo)


def _merge_sample(o_attn, bg, u, p1, p2, conv_w, ga, gc, x2d, mod_spec, g1, n2, sc2, sh2, wa, wc, wo, s_len):
    n = x2d.shape[0]
    tm = TM_MERGE
    row = lambda w: pl.BlockSpec((tm, w), lambda i: (i, 0))
    ins, outs = _merge_common_specs(tm, mod_spec)
    return pl.pallas_call(
        functools.partial(_merge_sample_kernel, s_len=s_len),
        out_shape=[jax.ShapeDtypeStruct((n, D_MODEL), F32), jax.ShapeDtypeStruct((D_MODEL // 2, n), U32)],
        grid=(n // tm,),
        in_specs=[row(GROUP_WIDTH)] + [row(CONV_WIDTH)] * 4 + [pl.BlockSpec((CONV_K, CONV_WIDTH), lambda i: (0, 0))] + ins,
        out_specs=outs,
        scratch_shapes=[pltpu.VMEM((SUBLANES + tm, CONV_WIDTH), F32)],
        compiler_params=_cparams(("arbitrary",)),
        name="merge_sample",
    )(o_attn, bg, u, p1, p2, conv_w, ga, gc, x2d, g1, n2, sc2, sh2, wa, wc, wo)


def _oddeven_merge_sort_pairs(n):
    pairs = []
    p = 1
    while p < n:
        k = p
        while k >= 1:
            for j in range(k % p, n - k, 2 * k):
                for i in range(min(k, n - j - k)):
                    if (i + j) // (2 * p) == (i + j + k) // (2 * p):
                        pairs.append((i + j, i + j + k))
            k //= 2
        p *= 2
    return pairs


_SORT16 = _oddeven_merge_sort_pairs(PEER_TOPK)


def _cmpx(x, a, b):
    hi, lo = jnp.maximum(x[a], x[b]), jnp.minimum(x[a], x[b])
    x[a], x[b] = hi, lo


def _bitonic_clean(x):
    stride = PEER_TOPK // 2
    while stride >= 1:
        for i in range(PEER_TOPK):
            if i & stride == 0:
                _cmpx(x, i, i + stride)
        stride //= 2
    return x


def _merge_top16(a, b):
    c = []
    for j in range(PEER_TOPK):
        jb = PEER_TOPK - 1 - j
        c.append(jnp.maximum(a[j], b[jb]) if jb < len(b) else a[j])
    return _bitonic_clean(c)


def _top16_over_keys(s):
    x = [s[j * SUBLANES:(j + 1) * SUBLANES, :] for j in range(N_KEYS // SUBLANES)]
    for a, b in _SORT16:
        _cmpx(x, a, b)
    shift = SUBLANES // 2
    while shift >= 1:
        x = _merge_top16(x, [pltpu.roll(v, shift, 0) for v in x])
        shift //= 2
    return x


def _route_kernel(h_ref, wq_ref, k1_ref, k2_ref, cnt_ref, m1_ref, r2_ref, e2_ref):
    tk = h_ref.shape[1]
    qt = jnp.dot(_unpack(wq_ref[...]), _unpack(h_ref[...]), preferred_element_type=F32).astype(BF16)
    k1, k2 = k1_ref[...], k2_ref[...]
    sub = lax.broadcasted_iota(jnp.int32, (SUBLANES, tk), 0)
    half = N_KEYS
    s1s, s2s = [], []
    v1 = [jnp.zeros((SUBLANES, tk), F32) for _ in range(PEER_TOPK)]
    v2 = [jnp.zeros((SUBLANES, tk), F32) for _ in range(PEER_TOPK)]
    for h in range(PEER_HEADS):
        base = h * 2 * half
        s1 = jnp.dot(k1, qt[base:base + half], preferred_element_type=F32)
        s2 = jnp.dot(k2, qt[base + half:base + 2 * half], preferred_element_type=F32)
        s1s.append(s1)
        s2s.append(s2)
        t1 = _top16_over_keys(s1)
        t2 = _top16_over_keys(s2)
        for j in range(PEER_TOPK):
            v1[j] = jnp.where(sub == h, t1[j], v1[j])
            v2[j] = jnp.where(sub == h, t2[j], v2[j])
    psum = {}
    for a in range(PEER_TOPK):
        for b in range(PEER_TOPK // (a + 1)):
            psum[a, b] = v1[a] + v2[b]
    lists = [[psum[a, b] for b in range(PEER_TOPK // (a + 1))] for a in range(PEER_TOPK // 2)]
    lists.append([psum[a, 0] for a in range(PEER_TOPK // 2, PEER_TOPK)])
    top = lists[0]
    for other in lists[1:]:
        top = _merge_top16(top, other)
    tau = top[PEER_TOPK - 1]
    z = jnp.ones((SUBLANES, tk), F32)
    for j in range(1, PEER_TOPK):
        z = z + jnp.exp(top[j] - top[0])
    rz = 1.0 / z
    x1 = []
    for b in range(PEER_TOPK):
        x = jnp.full((SUBLANES, tk), jnp.inf, F32)
        for a in range(PEER_TOPK // (b + 1)):
            x = jnp.where(psum[a, b] >= tau, v1[a], x)
        x1.append(x)
    for h in range(PEER_HEADS):
        s1, s2 = s1s[h], s2s[h]
        row = lambda v: v[h:h + 1, :]
        cnt = jnp.zeros_like(s1)
        rank = jnp.full_like(s2, float(PEER_TOPK))
        for j in range(PEER_TOPK):
            cnt = jnp.where(s1 >= row(x1[j]), float(j + 1), cnt)
        for j in range(PEER_TOPK - 1, -1, -1):
            rank = jnp.where(s2 >= row(v2[j]), float(j), rank)
        m1 = jnp.exp(s1 - row(v1[0])) * row(rz)
        cnt_ref[:, h, :, :] = cnt.reshape(N_KEYS // SUBLANES, SUBLANES, tk)
        m1_ref[:, h, :, :] = m1.reshape(N_KEYS // SUBLANES, SUBLANES, tk)
        half_rows = slice(h * N_KEYS // 2, (h + 1) * N_KEYS // 2)
        r2_ref[half_rows, :] = pltpu.bitcast(rank.astype(BF16), U32)
        e2_ref[half_rows, :] = pltpu.bitcast(jnp.exp(s2 - row(v2[0])).astype(BF16), U32)


def _route(h2t, wq_t, k1, k2):
    n = h2t.shape[1]
    tk = TK_ROUTE
    a8 = N_KEYS // SUBLANES
    s14 = jax.ShapeDtypeStruct((a8, PEER_HEADS, SUBLANES, n), F32)
    s2d = jax.ShapeDtypeStruct((PEER_HEADS * N_KEYS // 2, n), U32)
    spec4 = pl.BlockSpec((a8, PEER_HEADS, SUBLANES, tk), lambda i: (0, 0, 0, i))
    spec2 = pl.BlockSpec((PEER_HEADS * N_KEYS // 2, tk), lambda i: (0, i))
    return pl.pallas_call(
        _route_kernel,
        out_shape=[s14, s14, s2d, s2d],
        grid=(n // tk,),
        in_specs=[pl.BlockSpec((D_MODEL // 2, tk), lambda i: (0, i)),
                  pl.BlockSpec(wq_t.shape, lambda i: (0, 0)),
                  pl.BlockSpec(k1.shape, lambda i: (0, 0)),
                  pl.BlockSpec(k2.shape, lambda i: (0, 0))],
        out_specs=[spec4, spec4, spec2, spec2],
        compiler_params=_cparams(("arbitrary",)),
        name="peer_route",
    )(h2t, wq_t, k1, k2)


FFN_MXU_TILE = 256
FFN_VPU_TILE = LANES
BF16_ROWS = 16


def _ffn_kernel(h_ref, u_ref, vt_ref, cnt_ref, m1_ref, r2_ref, e2_ref, o_ref, gelu_ref, coef_ref):
    tm = h_ref.shape[1]
    n_bt = N_KEYS // BF16_ROWS

    @pl.when(pl.program_id(1) == 0)
    def _():
        o_ref[...] = jnp.zeros_like(o_ref)

    def row16(ref, h, al, ls):
        return jnp.broadcast_to(ref[h, al:al + 1, ls], (BF16_ROWS, FFN_VPU_TILE)).astype(BF16)

    def mxu_cols(mt):
        return slice(mt * FFN_MXU_TILE, (mt + 1) * FFN_MXU_TILE)

    def build_gates(mt):
        for vt in range(FFN_MXU_TILE // FFN_VPU_TILE):
            lo = mt * FFN_MXU_TILE + vt * FFN_VPU_TILE
            ls = slice(lo, lo + FFN_VPU_TILE)
            for a0 in range(0, EB_FFN // N_KEYS, 2):
                acc = [[jnp.zeros((BF16_ROWS, FFN_VPU_TILE), BF16) for _ in range(n_bt)] for _ in range(2)]
                for h in range(PEER_HEADS):
                    cnts = [row16(cnt_ref, h, a0 + k, ls) for k in range(2)]
                    m1s = [row16(m1_ref, h, a0 + k, ls) for k in range(2)]
                    for bt in range(n_bt):
                        rs = slice((h * N_KEYS + bt * BF16_ROWS) // 2, (h * N_KEYS + (bt + 1) * BF16_ROWS) // 2)
                        rank, e2 = _unpack(r2_ref[rs, ls]), _unpack(e2_ref[rs, ls])
                        for k in range(2):
                            acc[k][bt] = acc[k][bt] + jnp.where(rank < cnts[k], e2, jnp.zeros_like(e2)) * m1s[k]
                for k in range(2):
                    for bt in range(n_bt):
                        es = slice((a0 + k) * N_KEYS + bt * BF16_ROWS, (a0 + k) * N_KEYS + (bt + 1) * BF16_ROWS)
                        coef_ref[es, ls] = acc[k][bt]

    def up_proj(mt):
        x = jnp.dot(_unpack(u_ref[...]), _unpack(h_ref[:, mxu_cols(mt)]), preferred_element_type=F32)
        k0 = (2.0 / jnp.pi) ** 0.5
        hx = 0.5 * x
        gelu_ref[:, mxu_cols(mt)] = hx + hx * jnp.tanh(x * (k0 + (k0 * 0.044715) * (x * x)))

    def down_proj(mt):
        ms = mxu_cols(mt)
        coef_ref[:, ms] = (coef_ref[:, ms].astype(F32) * gelu_ref[:, ms]).astype(BF16)
        o_ref[:, ms] += jnp.dot(_unpack(vt_ref[...]), coef_ref[:, ms], preferred_element_type=F32)

    n_mt = tm // FFN_MXU_TILE
    build_gates(0)
    up_proj(0)
    for mt in range(n_mt):
        if mt + 1 < n_mt:
            build_gates(mt + 1)
            up_proj(mt + 1)
        down_proj(mt)


def _ffn(h2t, u_b, vt_b, cnt, m1, r2, e2):
    n = h2t.shape[1]
    tm, eb = TM_FFN, EB_FFN
    n_exp = 2 * u_b.shape[0]
    spec4 = pl.BlockSpec((None, PEER_HEADS, SUBLANES, tm), lambda i, e: (e, 0, 0, i))
    spec2 = pl.BlockSpec((PEER_HEADS * N_KEYS // 2, tm), lambda i, e: (0, i))
    return pl.pallas_call(
        _ffn_kernel,
        out_shape=jax.ShapeDtypeStruct((D_MODEL, n), F32),
        grid=(n // tm, n_exp // eb),
        in_specs=[pl.BlockSpec((D_MODEL // 2, tm), lambda i, e: (0, i)),
                  pl.BlockSpec((eb // 2, D_MODEL), lambda i, e: (e, 0)),
                  pl.BlockSpec((D_MODEL // 2, eb), lambda i, e: (0, e)),
                  spec4, spec4, spec2, spec2],
        out_specs=pl.BlockSpec((D_MODEL, tm), lambda i, e: (0, i)),
        scratch_shapes=[pltpu.VMEM((eb, tm), F32), pltpu.VMEM((eb, tm), BF16)],
        compiler_params=_cparams(("arbitrary", "arbitrary")),
        name="peer_ffn",
    )(h2t, u_b, vt_b, cnt, m1, r2, e2)


def _final_kernel(p_ref, x_ref, g2_ref, nf_ref, y_ref):
    x2 = x_ref[...] + g2_ref[...] * p_ref[...].T
    ms = jnp.mean(x2 * x2, axis=-1, keepdims=True)
    y_ref[...] = x2 * lax.rsqrt(ms + NORM_EPS) * nf_ref[...]


def _final(peer_t, col_off, x1, mod_spec, g2, nf):
    n = x1.shape[0]
    tm = TM_FINAL
    off = col_off // tm
    return pl.pallas_call(
        _final_kernel,
        out_shape=jax.ShapeDtypeStruct((n, D_MODEL), F32),
        grid=(n // tm,),
        in_specs=[pl.BlockSpec((D_MODEL, tm), lambda i: (0, i + off)),
                  pl.BlockSpec((tm, D_MODEL), lambda i: (i, 0)),
                  mod_spec,
                  pl.BlockSpec((1, D_MODEL), lambda i: (0, 0))],
        out_specs=pl.BlockSpec((tm, D_MODEL), lambda i: (i, 0)),
        compiler_params=_cparams(("arbitrary",)),
        name="final",
    )(peer_t, x1, g2, nf)


def _rope_tables(pos):
    half = HEAD_DIM // 2
    inv = ROPE_THETA ** (-jnp.arange(half, dtype=F32) / half)
    ang = pos.astype(F32)[:, None] * inv[None, :]
    cos, sin = jnp.cos(ang), jnp.sin(ang)
    zero = jnp.zeros_like(sin)
    reps = LANES // HEAD_DIM
    cos_t = jnp.tile(jnp.concatenate([cos, cos], axis=1), (1, reps))
    sa_t = jnp.tile(jnp.concatenate([-sin, zero], axis=1), (1, reps))
    sb_t = jnp.tile(jnp.concatenate([zero, sin], axis=1), (1, reps))
    return cos_t, sa_t, sb_t


def kernel(x_prompt, x_sample, cache_kv0, cache_kv1, cache_kv2, state_conv, c_prompt, c_sample,
           norm1_g, norm2_g, norm_f_g, w_ada, b_ada, w_in, conv_w, w_attn_o, w_conv_o, w_o,
           w_query, sub_keys, expert_u, expert_v):
    batch, seq, _ = x_prompt.shape
    n_seq, s_len, _ = x_sample.shape
    depth = w_in.shape[0]
    assert depth == 1
    assert tuple(c.shape[2] for c in (cache_kv0, cache_kv1, cache_kv2)) == WINDOWS
    assert s_len == SUBLANES and seq % (ATTN_BLK * DILATIONS[2]) == 0
    n_p, n_s = batch * seq, n_seq * s_len
    l = 0

    w_in_b = w_in[l].astype(BF16)
    wa_b, wc_b, wo_b = w_attn_o[l].astype(BF16), w_conv_o[l].astype(BF16), w_o[l].astype(BF16)
    wq_t = _pack_rows(w_query[l].T.astype(BF16))
    k1_b, k2_b = sub_keys[l, 0].astype(BF16), sub_keys[l, 1].astype(BF16)
    u_b = _pack_rows(expert_u[l].astype(BF16))
    vt_b = _pack_rows(expert_v[l].T.astype(BF16))
    n1, n2, nf = norm1_g[l][None, :], norm2_g[l][None, :], norm_f_g[None, :]

    n_c = batch + n_seq
    c_all = jnp.concatenate([c_prompt, c_sample, jnp.zeros((-n_c % SUBLANES, D_MODEL), F32)], axis=0)
    mod = _ada(c_all, w_ada[l], b_ada[l])
    mod_p = [m[:, None, :] for m in jnp.split(mod[:batch], 6, axis=-1)]
    mod_s = jnp.split(jnp.repeat(mod[batch:n_c], s_len, axis=0), 6, axis=-1)

    def seq_spec(tm):
        return pl.BlockSpec((None, 1, D_MODEL), lambda i: (i // (seq // tm), 0, 0))

    def tok_spec(tm):
        return pl.BlockSpec((tm, D_MODEL), lambda i: (i, 0))

    tabs_p = _rope_tables(jnp.arange(seq))
    tabs_s = [jnp.tile(t, (n_seq, 1)) for t in _rope_tables(PAST_LEN + jnp.arange(s_len))]
    tab_p_spec = pl.BlockSpec((TM_PROJ, LANES), lambda i: (i % (seq // TM_PROJ), 0))
    tab_s_spec = pl.BlockSpec((TM_PROJ, LANES), lambda i: (i, 0))

    xp = x_prompt.reshape(n_p, D_MODEL)
    xs = x_sample.reshape(n_s, D_MODEL)
    qp, kp, vp, bgp, up, gap, gcp = _inproj(xp, seq_spec(TM_PROJ), (mod_p[1], mod_p[0]), tab_p_spec, tabs_p, n1, w_in_b)
    qs, ks, vs, bgs, us, gas, gcs = _inproj(xs, tok_spec(TM_PROJ), (mod_s[1], mod_s[0]), tab_s_spec, tabs_s, n1, w_in_b)

    attn_p = [_pattn(qp, kp, vp, g, batch, seq) for g in range(N_GROUPS)]
    caches = (cache_kv0[l], cache_kv1[l], cache_kv2[l])
    attn_s = _sattn(qs, ks, vs, caches, n_seq, s_len)

    x1p, h2tp = _merge_prompt(attn_p, bgp, up, conv_w[l], gap, gcp, xp, seq_spec(TM_MERGE),
                              mod_p[2], n2, mod_p[4], mod_p[3], wa_b, wc_b, wo_b, seq)
    st = state_conv[l]
    p1 = jnp.pad(st[:, 1:2], ((0, 0), (0, s_len - 1), (0, 0))).reshape(n_s, CONV_WIDTH)
    p2 = jnp.pad(st, ((0, 0), (0, s_len - 2), (0, 0))).reshape(n_s, CONV_WIDTH)
    x1s, h2ts = _merge_sample(attn_s, bgs, us, p1, p2, conv_w[l], gas, gcs, xs, tok_spec(TM_MERGE),
                              mod_s[2], n2, mod_s[4], mod_s[3], wa_b, wc_b, wo_b, s_len)

    h2t = jnp.concatenate([h2tp, h2ts], axis=1)
    cnt, m1, r2, e2 = _route(h2t, wq_t, k1_b, k2_b)
    peer_t = _ffn(h2t, u_b, vt_b, cnt, m1, r2, e2)

    yp = _final(peer_t, 0, x1p, seq_spec(TM_FINAL), mod_p[5], nf).reshape(batch, seq, D_MODEL)
    ys = _final(peer_t, n_p, x1s, tok_spec(TM_FINAL), mod_s[5], nf).reshape(n_seq, s_len, D_MODEL)

    def kv_tail(k2d, v2d, g, rows, n_rows):
        gs = slice(g * GROUP_WIDTH, (g + 1) * GROUP_WIDTH)
        k4 = k2d.reshape(rows, n_rows, ATTN_WIDTH)[:, :, gs].reshape(rows, n_rows, HEADS_PER_GROUP, HEAD_DIM)
        v4 = v2d.reshape(rows, n_rows, ATTN_WIDTH)[:, :, gs].reshape(rows, n_rows, HEADS_PER_GROUP, HEAD_DIM)
        return k4, v4

    kv_p, kv_s = [], []
    for g in range(N_GROUPS):
        lp = min(WINDOWS[g], seq)
        k4, v4 = kv_tail(kp, vp, g, batch, seq)
        kv_p.append(jnp.stack([k4[:, seq - lp:], v4[:, seq - lp:]], axis=2)[None])
        k4, v4 = kv_tail(ks, vs, g, n_seq, s_len)
        kv_s.append(jnp.stack([k4, v4], axis=2)[None])
    conv_p = up.reshape(batch, seq, CONV_WIDTH)[:, seq - (CONV_K - 1):][None]
    conv_s = jnp.concatenate([st, us.reshape(n_seq, s_len, CONV_WIDTH)], axis=1)[:, s_len:][None]
    return (yp, ys, kv_p[0], kv_p[1], kv_p[2], conv_p, kv_s[0], kv_s[1], kv_s[2], conv_s)
```

```python
import functools

import jax
import jax.numpy as jnp
from jax import lax
from jax.experimental import pallas as pl
from jax.experimental.pallas import tpu as pltpu

F32 = jnp.float32
BF16 = jnp.bfloat16
U32 = jnp.uint32

D_MODEL = 1024
HEAD_DIM = 64
HEADS_PER_GROUP = 4
WINDOWS = (128, 512, 2048)
DILATIONS = (1, 4, 16)
N_GROUPS = 3
GROUP_WIDTH = HEADS_PER_GROUP * HEAD_DIM
ATTN_WIDTH = N_GROUPS * GROUP_WIDTH
CONV_WIDTH = 768
CONV_K = 3
IN_WIDTH = 3 * ATTN_WIDTH + 3 * CONV_WIDTH + 2 * D_MODEL
ATTN_SCALE = HEAD_DIM ** -0.5
ROPE_THETA = 10000.0
N_KEYS = 128
PEER_HEADS = 8
PEER_TOPK = 16
NORM_EPS = 1e-6
N_BACK = 128
PAST_LEN = 8192

LANES = 128
SUBLANES = 8
VMEM_LIMIT = 52 * 1024 * 1024

TM_PROJ = 256
TM_MERGE = 256
TM_FINAL = 256
TK_ROUTE = 256
TM_FFN = 1024
EB_FFN = 1024
ATTN_BLK = 128
ATTN_STEP = 512


def _unpack(packed):
    return pltpu.bitcast(packed, BF16)


def _pack_kernel(w_ref, o_ref, *, transpose):
    w = w_ref[...]
    if transpose:
        w = w.T
    o_ref[...] = pltpu.bitcast(w.astype(BF16), U32)


def _pack_weight(w, transpose=False):
    r, c = w.shape
    tr = next(t for t in (1024, 512, 256, 128) if r % t == 0)
    tc = next(t for t in (1024, 512, 256, 128) if c % t == 0)
    if transpose:
        out_shape, out_spec = (c // 2, r), pl.BlockSpec((tc // 2, tr), lambda i, j: (j, i))
    else:
        out_shape, out_spec = (r // 2, c), pl.BlockSpec((tr // 2, tc), lambda i, j: (i, j))
    return pl.pallas_call(
        functools.partial(_pack_kernel, transpose=transpose),
        out_shape=jax.ShapeDtypeStruct(out_shape, U32),
        grid=(r // tr, c // tc),
        in_specs=[pl.BlockSpec((tr, tc), lambda i, j: (i, j))],
        out_specs=out_spec,
        compiler_params=_cparams(("arbitrary", "arbitrary")),
        name="pack_weight",
    )(w)


def _cparams(sem):
    return pltpu.CompilerParams(dimension_semantics=sem, vmem_limit_bytes=VMEM_LIMIT)


def _ada_kernel(c_ref, w_ref, b_ref, o_ref):
    c = c_ref[...]
    a = (c * jax.nn.sigmoid(c)).astype(BF16)
    o_ref[...] = jnp.dot(a, w_ref[...].astype(BF16), preferred_element_type=F32) + b_ref[...]


def _ada(c_all, w_ada, b_ada):
    rows = c_all.shape[0]
    n_out = w_ada.shape[1]
    tn = 1024
    return pl.pallas_call(
        _ada_kernel,
        out_shape=jax.ShapeDtypeStruct((rows, n_out), F32),
        grid=(n_out // tn,),
        in_specs=[pl.BlockSpec((rows, D_MODEL), lambda j: (0, 0)),
                  pl.BlockSpec((D_MODEL, tn), lambda j: (0, j)),
                  pl.BlockSpec((1, tn), lambda j: (0, j))],
        out_specs=pl.BlockSpec((rows, tn), lambda j: (0, j)),
        compiler_params=_cparams(("arbitrary",)),
        name="adaln",
    )(c_all, w_ada, b_ada.reshape(1, n_out))


def _rmsnorm_mod(x, g, scale, shift):
    ms = jnp.mean(x * x, axis=-1, keepdims=True)
    y = x * lax.rsqrt(ms + NORM_EPS) * g
    return y * (1.0 + scale) + shift


def _inproj_kernel(x_ref, sc_ref, sh_ref, n1_ref, cos_ref, sa_ref, sb_ref, w_ref, *refs, by_residue):
    n_qkv = 3 * N_GROUPS if by_residue else 3
    bg_ref, u_ref, ga_ref, gc_ref, zbuf = refs[n_qkv:]
    n_chunks, tm = zbuf.shape[0], zbuf.shape[1]
    per_group = GROUP_WIDTH // LANES
    h = _rmsnorm_mod(x_ref[...], n1_ref[...], sc_ref[...], sh_ref[...]).astype(BF16)

    def proj(lo, hi):
        return jnp.dot(h, _unpack(w_ref[:, lo:hi]), preferred_element_type=F32)

    cos, sa, sb = cos_ref[...], sa_ref[...], sb_ref[...]

    def stage(z, rope, mult):
        for c in range(n_chunks):
            zc = z[:, c * LANES:(c + 1) * LANES]
            if rope:
                zc = zc * cos + pltpu.roll(zc, LANES - 32, 1) * sa + pltpu.roll(zc, 32, 1) * sb
            zbuf[c] = zc * mult if mult != 1.0 else zc

    def emit(which):
        for c in range(n_chunks):
            if not by_residue:
                refs[which][:, c * LANES:(c + 1) * LANES] = zbuf[c].astype(refs[which].dtype)
                continue
            g, cl = c // per_group, c % per_group
            ref, d = refs[which * N_GROUPS + g], DILATIONS[g]
            for r in range(d):
                ref[r, :, cl * LANES:(cl + 1) * LANES] = zbuf[c, pl.ds(r, tm // d, stride=d), :].astype(ref.dtype)

    o = 0
    stage(proj(o, o + ATTN_WIDTH), True, ATTN_SCALE)
    emit(0)
    o += ATTN_WIDTH
    stage(proj(o, o + ATTN_WIDTH), True, 1.0)
    emit(1)
    o += ATTN_WIDTH
    stage(proj(o, o + ATTN_WIDTH), False, 1.0)
    emit(2)
    o += ATTN_WIDTH
    bg_ref[...] = proj(o, o + CONV_WIDTH)
    o += CONV_WIDTH
    cg = proj(o, o + CONV_WIDTH)
    o += CONV_WIDTH
    u_ref[...] = cg * proj(o, o + CONV_WIDTH)
    o += CONV_WIDTH
    ga_ref[...] = proj(o, o + D_MODEL)
    o += D_MODEL
    gc_ref[...] = proj(o, o + D_MODEL)


def _inproj(x2d, mod_specs, mods, tab_spec, tabs, n1, w_in_p, residue_seq=None):
    n = x2d.shape[0]
    tm = TM_PROJ
    row = lambda w: pl.BlockSpec((tm, w), lambda i: (i, 0))
    const = lambda s: pl.BlockSpec(s, lambda i: (0, 0))
    if residue_seq is None:
        qkv_shapes = [jax.ShapeDtypeStruct((n, ATTN_WIDTH), dt) for dt in (BF16, F32, F32)]
        qkv_specs = [row(ATTN_WIDTH)] * 3
    else:
        bps = residue_seq // tm
        qkv_shapes, qkv_specs = [], []
        for dt in (BF16, F32, F32):
            for d in DILATIONS:
                qkv_shapes.append(jax.ShapeDtypeStruct((n // residue_seq, d, residue_seq // d, GROUP_WIDTH), dt))
                qkv_specs.append(pl.BlockSpec((None, d, tm // d, GROUP_WIDTH), lambda i: (i // bps, 0, i % bps, 0)))
    outs = qkv_shapes + [jax.ShapeDtypeStruct((n, CONV_WIDTH), F32),
                         jax.ShapeDtypeStruct((n, CONV_WIDTH), F32),
                         jax.ShapeDtypeStruct((n, D_MODEL), F32),
                         jax.ShapeDtypeStruct((n, D_MODEL), F32)]
    return pl.pallas_call(
        functools.partial(_inproj_kernel, by_residue=residue_seq is not None),
        out_shape=outs,
        grid=(n // tm,),
        in_specs=[row(D_MODEL), mod_specs, mod_specs, const((1, D_MODEL)),
                  tab_spec, tab_spec, tab_spec, const(w_in_p.shape)],
        out_specs=qkv_specs + [row(CONV_WIDTH)] * 2 + [row(D_MODEL)] * 2,
        scratch_shapes=[pltpu.VMEM((ATTN_WIDTH // LANES, tm, LANES), F32)],
        compiler_params=_cparams(("arbitrary",)),
        name="inproj",
    )(x2d, mods[0], mods[1], n1, tabs[0], tabs[1], tabs[2], w_in_p)


def _pattn_kernel(q_ref, kp_ref, kc_ref, vp_ref, vc_ref, o_ref, l_ref):
    nb = pl.program_id(2)
    k = jnp.concatenate([kp_ref[...], kc_ref[...]], axis=0).astype(BF16)
    v = jnp.concatenate([vp_ref[...], vc_ref[...]], axis=0).astype(BF16)
    qi = lax.broadcasted_iota(jnp.int32, (ATTN_BLK, 2 * ATTN_BLK), 0)
    ki = lax.broadcasted_iota(jnp.int32, (ATTN_BLK, 2 * ATTN_BLK), 1)
    dist = qi + ATTN_BLK - ki
    band = (dist >= 0) & (dist <= N_BACK)
    band0 = band & (ki >= jnp.where(nb > 0, 0, ATTN_BLK))
    for sb in range(ATTN_STEP // ATTN_BLK):
        rows = slice(sb * ATTN_BLK, (sb + 1) * ATTN_BLK)
        keys = slice(sb * ATTN_BLK, (sb + 2) * ATTN_BLK)
        q = q_ref[rows, :]
        valid = band0 if sb == 0 else band
        for h in range(HEADS_PER_GROUP):
            sl = slice(h * HEAD_DIM, (h + 1) * HEAD_DIM)
            s = lax.dot_general(q[:, sl], k[keys, sl], (((1,), (1,)), ((), ())), preferred_element_type=F32)
            s = jnp.where(valid, s, -jnp.inf)
            m = jnp.max(s, axis=-1, keepdims=True)
            p = jnp.exp(s - m)
            den = jnp.sum(p, axis=-1, keepdims=True)
            o = jnp.dot(p.astype(BF16), v[keys, sl], preferred_element_type=F32) / den
            o_ref[rows, sl] = o
            l_ref[rows, sl] = jnp.broadcast_to(m + jnp.log(den), (ATTN_BLK, HEAD_DIM))


def _pattn(q, k, v, g):
    batch, d, n, _ = q.shape
    per_blk = ATTN_STEP // ATTN_BLK
    cur = pl.BlockSpec((None, None, ATTN_STEP, GROUP_WIDTH), lambda b, r, nb: (b, r, nb, 0))
    prev = pl.BlockSpec((None, None, ATTN_BLK, GROUP_WIDTH),
                        lambda b, r, nb: (b, r, jnp.maximum(nb * per_blk - 1, 0), 0))
    return pl.pallas_call(
        _pattn_kernel,
        out_shape=[jax.ShapeDtypeStruct((batch, d, n, GROUP_WIDTH), F32)] * 2,
        grid=(batch, d, n // ATTN_STEP),
        in_specs=[cur, prev, cur, prev, cur],
        out_specs=[cur, cur],
        compiler_params=_cparams(("arbitrary",) * 3),
        name=f"pattn{g}",
    )(q, k, k, v, v)


def _sattn_kernel(q_ref, k_ref, v_ref, c0_ref, c1_ref, c2_ref, o_ref, *, s_len):
    q = q_ref[...].astype(F32)
    kn = k_ref[...]
    vn = v_ref[...]
    rows = HEADS_PER_GROUP * s_len
    lane_head = lax.broadcasted_iota(jnp.int32, (s_len, GROUP_WIDTH), 1) // HEAD_DIM
    pad = jnp.zeros((LANES - s_len, GROUP_WIDTH), F32)
    nt_dims = (((1,), (1,)), ((), ()))
    ms, ls, os_ = [], [], []
    for g, cref in enumerate((c0_ref, c1_ref, c2_ref)):
        d, win = DILATIONS[g], WINDOWS[g]
        gs = slice(g * GROUP_WIDTH, (g + 1) * GROUP_WIDTH)
        qg = q[:, gs]
        qexp = jnp.concatenate(
            [jnp.where(lane_head == h, qg, 0.0) for h in range(HEADS_PER_GROUP)], axis=0).astype(BF16)
        s_c = jnp.dot(qexp, cref[0].astype(BF16), preferred_element_type=F32)
        s_n = lax.dot_general(qexp, jnp.concatenate([kn[:, gs], pad], axis=0).astype(BF16), nt_dims,
                              preferred_element_type=F32)
        back_c = (win + lax.broadcasted_iota(jnp.int32, (rows, win), 0) % s_len
                  - lax.broadcasted_iota(jnp.int32, (rows, win), 1))
        col_n = lax.broadcasted_iota(jnp.int32, (rows, LANES), 1)
        back_n = lax.broadcasted_iota(jnp.int32, (rows, LANES), 0) % s_len - col_n
        s_c = jnp.where((back_c <= win) & ((back_c & (d - 1)) == 0), s_c, -jnp.inf)
        s_n = jnp.where((back_n >= 0) & ((back_n & (d - 1)) == 0) & (col_n < s_len), s_n, -jnp.inf)
        m = jnp.maximum(jnp.max(s_c, axis=-1, keepdims=True), jnp.max(s_n, axis=-1, keepdims=True))
        p_c = jnp.exp(s_c - m)
        p_n = jnp.exp(s_n - m)
        ls.append(jnp.sum(p_c, axis=-1, keepdims=True) + jnp.sum(p_n, axis=-1, keepdims=True))
        ms.append(m)
        o = lax.dot_general(p_c.astype(BF16), cref[1].astype(BF16), nt_dims, preferred_element_type=F32)
        o = o + jnp.dot(p_n.astype(BF16), jnp.concatenate([vn[:, gs], pad], axis=0).astype(BF16),
                        preferred_element_type=F32)
        os_.append(o)
    mm = jnp.maximum(jnp.maximum(ms[0], ms[1]), ms[2])
    num = jnp.zeros((rows, GROUP_WIDTH), F32)
    den = jnp.zeros((rows, 1), F32)
    for g in range(N_GROUPS):
        w = jnp.exp(ms[g] - mm)
        num = num + w * os_[g]
        den = den + w * ls[g]
    full = num / den
    out = jnp.zeros((s_len, GROUP_WIDTH), F32)
    for h in range(HEADS_PER_GROUP):
        out = out + jnp.where(lane_head == h, full[h * s_len:(h + 1) * s_len], 0.0)
    o_ref[...] = out


def _sattn(q, k, v, caches, n_seq, s_len):
    q3 = q.reshape(n_seq, s_len, ATTN_WIDTH)
    k3 = k.reshape(n_seq, s_len, ATTN_WIDTH)
    v3 = v.reshape(n_seq, s_len, ATTN_WIDTH)
    cts = [jnp.transpose(c, (0, 2, 3, 4, 1)).reshape(n_seq, 2, GROUP_WIDTH, c.shape[1]) for c in caches]
    tok = pl.BlockSpec((None, s_len, ATTN_WIDTH), lambda b: (b, 0, 0))
    out = pl.pallas_call(
        functools.partial(_sattn_kernel, s_len=s_len),
        out_shape=jax.ShapeDtypeStruct((n_seq, s_len, GROUP_WIDTH), F32),
        grid=(n_seq,),
        in_specs=[tok, tok, tok] + [pl.BlockSpec((None, 2, GROUP_WIDTH, w), lambda b: (b, 0, 0, 0)) for w in WINDOWS],
        out_specs=pl.BlockSpec((None, s_len, GROUP_WIDTH), lambda b: (b, 0, 0)),
        compiler_params=_cparams(("arbitrary",)),
        name="sattn",
    )(q3, k3, v3, *cts)
    return out.reshape(n_seq * s_len, GROUP_WIDTH)


def _merge_tail(o_attn, bg, yc, ga_ref, gc_ref, x_ref, g1_ref, n2_ref, sc2_ref, sh2_ref,
                wa_ref, wc_ref, wo_ref, x1_ref, h2t_ref):
    a_out = jnp.dot(o_attn.astype(BF16), _unpack(wa_ref[...]), preferred_element_type=F32)
    c_out = jnp.dot((bg * yc).astype(BF16), _unpack(wc_ref[...]), preferred_element_type=F32)
    mix = jax.nn.sigmoid(ga_ref[...]) * a_out + jax.nn.sigmoid(gc_ref[...]) * c_out
    mo = jnp.dot(mix.astype(BF16), _unpack(wo_ref[...]), preferred_element_type=F32)
    x1 = x_ref[...] + g1_ref[...] * mo
    x1_ref[...] = x1
    h2 = _rmsnorm_mod(x1, n2_ref[...], sc2_ref[...], sh2_ref[...])
    h2t_ref[...] = pltpu.bitcast(h2.T.astype(BF16), U32)


def _conv3(u, um1, um2, cw_ref):
    return cw_ref[0:1, :] * um2 + cw_ref[1:2, :] * um1 + cw_ref[2:3, :] * u


def _merge_prompt_kernel(o0_ref, o1_ref, o2_ref, l0_ref, l1_ref, l2_ref, bg_ref, u_ref, uh_ref, cw_ref,
                         ga_ref, gc_ref, x_ref, g1_ref, n2_ref, sc2_ref, sh2_ref, wa_ref, wc_ref, wo_ref,
                         x1_ref, h2t_ref, ubuf, *rowbufs, blocks_per_seq):
    tm = u_ref.shape[0]
    first = (pl.program_id(0) % blocks_per_seq) == 0

    def by_position(ref, buf):
        d, rows = ref.shape[0], ref.shape[1]
        if d == 1:
            return ref[0]
        for r in range(d):
            for c in range(GROUP_WIDTH // LANES):
                buf[c, pl.ds(r, rows, stride=d), :] = ref[r, :, c * LANES:(c + 1) * LANES]
        return jnp.concatenate([buf[c] for c in range(GROUP_WIDTH // LANES)], axis=1)

    o0, l0 = by_position(o0_ref, None), by_position(l0_ref, None)
    o1, l1 = by_position(o1_ref, rowbufs[0]), by_position(l1_ref, rowbufs[1])
    o2, l2 = by_position(o2_ref, rowbufs[2]), by_position(l2_ref, rowbufs[3])
    mm = jnp.maximum(jnp.maximum(l0, l1), l2)
    e0, e1, e2 = jnp.exp(l0 - mm), jnp.exp(l1 - mm), jnp.exp(l2 - mm)
    o_attn = (e0 * o0 + e1 * o1 + e2 * o2) / (e0 + e1 + e2)
    u = u_ref[...]
    ubuf[0:SUBLANES, :] = jnp.where(first, 0.0, uh_ref[...])
    ubuf[SUBLANES:SUBLANES + tm, :] = u
    yc = _conv3(u, ubuf[SUBLANES - 1:SUBLANES - 1 + tm, :], ubuf[SUBLANES - 2:SUBLANES - 2 + tm, :], cw_ref)
    _merge_tail(o_attn, bg_ref[...], yc, ga_ref, gc_ref, x_ref, g1_ref, n2_ref, sc2_ref, sh2_ref,
                wa_ref, wc_ref, wo_ref, x1_ref, h2t_ref)


def _merge_sample_kernel(oa_ref, bg_ref, u_ref, p1_ref, p2_ref, cw_ref,
                         ga_ref, gc_ref, x_ref, g1_ref, n2_ref, sc2_ref, sh2_ref, wa_ref, wc_ref, wo_ref,
                         x1_ref, h2t_ref, ubuf, *, s_len):
    tm = u_ref.shape[0]
    u = u_ref[...]
    ubuf[0:SUBLANES, :] = jnp.zeros((SUBLANES, CONV_WIDTH), F32)
    ubuf[SUBLANES:SUBLANES + tm, :] = u
    t = lax.broadcasted_iota(jnp.int32, (tm, CONV_WIDTH), 0) % s_len
    um1 = jnp.where(t < 1, p1_ref[...], ubuf[SUBLANES - 1:SUBLANES - 1 + tm, :])
    um2 = jnp.where(t < 2, p2_ref[...], ubuf[SUBLANES - 2:SUBLANES - 2 + tm, :])
    yc = _conv3(u, um1, um2, cw_ref)
    _merge_tail(oa_ref[...], bg_ref[...], yc, ga_ref, gc_ref, x_ref, g1_ref, n2_ref, sc2_ref, sh2_ref,
                wa_ref, wc_ref, wo_ref, x1_ref, h2t_ref)


def _merge_common_specs(tm, mod_spec):
    row = lambda w: pl.BlockSpec((tm, w), lambda i: (i, 0))
    const = lambda s: pl.BlockSpec(s, lambda i: (0, 0))
    ins = [row(D_MODEL), row(D_MODEL), row(D_MODEL), mod_spec, const((1, D_MODEL)), mod_spec, mod_spec,
           const((GROUP_WIDTH // 2, D_MODEL)), const((CONV_WIDTH // 2, D_MODEL)), const((D_MODEL // 2, D_MODEL))]
    outs = [row(D_MODEL), pl.BlockSpec((D_MODEL // 2, tm), lambda i: (0, i))]
    return ins, outs


def _merge_prompt(attn, bg, u, conv_w, ga, gc, x2d, mod_spec, g1, n2, sc2, sh2, wa, wc, wo, seq):
    n = x2d.shape[0]
    tm = TM_MERGE
    row = lambda w: pl.BlockSpec((tm, w), lambda i: (i, 0))
    halo = pl.BlockSpec((SUBLANES, CONV_WIDTH), lambda i: (jnp.maximum(i * (tm // SUBLANES) - 1, 0), 0))
    ins, outs = _merge_common_specs(tm, mod_spec)
    (o0, l0), (o1, l1), (o2, l2) = attn
    bps = seq // tm
    grp = [pl.BlockSpec((None, d, tm // d, GROUP_WIDTH), lambda i: (i // bps, 0, i % bps, 0)) for d in DILATIONS]
    return pl.pallas_call(
        functools.partial(_merge_prompt_kernel, blocks_per_seq=bps),
        out_shape=[jax.ShapeDtypeStruct((n, D_MODEL), F32), jax.ShapeDtypeStruct((D_MODEL // 2, n), U32)],
        grid=(n // tm,),
        in_specs=grp + grp + [row(CONV_WIDTH), row(CONV_WIDTH), halo,
                              pl.BlockSpec((CONV_K, CONV_WIDTH), lambda i: (0, 0))] + ins,
        out_specs=outs,
        scratch_shapes=[pltpu.VMEM((SUBLANES + tm, CONV_WIDTH), F32)] + [pltpu.VMEM((GROUP_WIDTH // LANES, tm, LANES), F32)] * 4,
        compiler_params=_cparams(("arbitrary",)),
        name="merge_prompt",
    )(o0, o1, o2, l0, l1, l2, bg, u, u, conv_w, ga, gc, x2d, g1, n2, sc2, sh2, wa, wc, wo)


def _merge_sample(o_attn, bg, u, p1, p2, conv_w, ga, gc, x2d, mod_spec, g1, n2, sc2, sh2, wa, wc, wo, s_len):
    n = x2d.shape[0]
    tm = TM_MERGE
    row = lambda w: pl.BlockSpec((tm, w), lambda i: (i, 0))
    ins, outs = _merge_common_specs(tm, mod_spec)
    return pl.pallas_call(
        functools.partial(_merge_sample_kernel, s_len=s_len),
        out_shape=[jax.ShapeDtypeStruct((n, D_MODEL), F32), jax.ShapeDtypeStruct((D_MODEL // 2, n), U32)],
        grid=(n // tm,),
        in_specs=[row(GROUP_WIDTH)] + [row(CONV_WIDTH)] * 4 + [pl.BlockSpec((CONV_K, CONV_WIDTH), lambda i: (0, 0))] + ins,
        out_specs=outs,
        scratch_shapes=[pltpu.VMEM((SUBLANES + tm, CONV_WIDTH), F32)],
        compiler_params=_cparams(("arbitrary",)),
        name="merge_sample",
    )(o_attn, bg, u, p1, p2, conv_w, ga, gc, x2d, g1, n2, sc2, sh2, wa, wc, wo)


def _oddeven_merge_sort_pairs(n):
    pairs = []
    p = 1
    while p < n:
        k = p
        while k >= 1:
            for j in range(k % p, n - k, 2 * k):
                for i in range(min(k, n - j - k)):
                    if (i + j) // (2 * p) == (i + j + k) // (2 * p):
                        pairs.append((i + j, i + j + k))
            k //= 2
        p *= 2
    return pairs


_SORT16 = _oddeven_merge_sort_pairs(PEER_TOPK)


def _cmpx(x, a, b):
    hi, lo = jnp.maximum(x[a], x[b]), jnp.minimum(x[a], x[b])
    x[a], x[b] = hi, lo


def _bitonic_clean(x):
    stride = PEER_TOPK // 2
    while stride >= 1:
        for i in range(PEER_TOPK):
            if i & stride == 0:
                _cmpx(x, i, i + stride)
        stride //= 2
    return x


def _merge_top16(a, b):
    c = []
    for j in range(PEER_TOPK):
        jb = PEER_TOPK - 1 - j
        c.append(jnp.maximum(a[j], b[jb]) if jb < len(b) else a[j])
    return _bitonic_clean(c)


def _top16_over_keys(s):
    x = [s[j * SUBLANES:(j + 1) * SUBLANES, :] for j in range(N_KEYS // SUBLANES)]
    for a, b in _SORT16:
        _cmpx(x, a, b)
    shift = SUBLANES // 2
    while shift >= 1:
        x = _merge_top16(x, [pltpu.roll(v, shift, 0) for v in x])
        shift //= 2
    return x


def _route_kernel(h_ref, wq_ref, k1_ref, k2_ref, cnt_ref, m1_ref, r2_ref, e2_ref):
    tk = h_ref.shape[1]
    qt = jnp.dot(_unpack(wq_ref[...]), _unpack(h_ref[...]), preferred_element_type=F32).astype(BF16)
    k1, k2 = k1_ref[...], k2_ref[...]
    sub = lax.broadcasted_iota(jnp.int32, (SUBLANES, tk), 0)
    half = N_KEYS
    s1s, s2s = [], []
    v1 = [jnp.zeros((SUBLANES, tk), F32) for _ in range(PEER_TOPK)]
    v2 = [jnp.zeros((SUBLANES, tk), F32) for _ in range(PEER_TOPK)]
    for h in range(PEER_HEADS):
        base = h * 2 * half
        s1 = jnp.dot(k1, qt[base:base + half], preferred_element_type=F32)
        s2 = jnp.dot(k2, qt[base + half:base + 2 * half], preferred_element_type=F32)
        s1s.append(s1)
        s2s.append(s2)
        t1 = _top16_over_keys(s1)
        t2 = _top16_over_keys(s2)
        for j in range(PEER_TOPK):
            v1[j] = jnp.where(sub == h, t1[j], v1[j])
            v2[j] = jnp.where(sub == h, t2[j], v2[j])
    psum = {}
    for a in range(PEER_TOPK):
        for b in range(PEER_TOPK // (a + 1)):
            psum[a, b] = v1[a] + v2[b]
    lists = [[psum[a, b] for b in range(PEER_TOPK // (a + 1))] for a in range(PEER_TOPK // 2)]
    lists.append([psum[a, 0] for a in range(PEER_TOPK // 2, PEER_TOPK)])
    top = lists[0]
    for other in lists[1:]:
        top = _merge_top16(top, other)
    tau = top[PEER_TOPK - 1]
    z = jnp.ones((SUBLANES, tk), F32)
    for j in range(1, PEER_TOPK):
        z = z + jnp.exp(top[j] - top[0])
    rz = 1.0 / z
    x1 = []
    for b in range(PEER_TOPK):
        x = jnp.full((SUBLANES, tk), jnp.inf, F32)
        for a in range(PEER_TOPK // (b + 1)):
            x = jnp.where(psum[a, b] >= tau, v1[a], x)
        x1.append(x)
    for h in range(PEER_HEADS):
        s1, s2 = s1s[h], s2s[h]
        row = lambda v: v[h:h + 1, :]
        cnt = jnp.zeros_like(s1)
        rank = jnp.full_like(s2, float(PEER_TOPK))
        for j in range(PEER_TOPK):
            cnt = jnp.where(s1 >= row(x1[j]), float(j + 1), cnt)
        for j in range(PEER_TOPK - 1, -1, -1):
            rank = jnp.where(s2 >= row(v2[j]), float(j), rank)
        m1 = jnp.exp(s1 - row(v1[0])) * row(rz)
        cnt_ref[:, h, :, :] = cnt.reshape(N_KEYS // SUBLANES, SUBLANES, tk)
        m1_ref[:, h, :, :] = m1.reshape(N_KEYS // SUBLANES, SUBLANES, tk)
        half_rows = slice(h * N_KEYS // 2, (h + 1) * N_KEYS // 2)
        r2_ref[half_rows, :] = pltpu.bitcast(rank.astype(BF16), U32)
        e2_ref[half_rows, :] = pltpu.bitcast(jnp.exp(s2 - row(v2[0])).astype(BF16), U32)


def _route(h2t, wq_t, k1, k2):
    n = h2t.shape[1]
    tk = TK_ROUTE
    a8 = N_KEYS // SUBLANES
    s14 = jax.ShapeDtypeStruct((a8, PEER_HEADS, SUBLANES, n), F32)
    s2d = jax.ShapeDtypeStruct((PEER_HEADS * N_KEYS // 2, n), U32)
    spec4 = pl.BlockSpec((a8, PEER_HEADS, SUBLANES, tk), lambda i: (0, 0, 0, i))
    spec2 = pl.BlockSpec((PEER_HEADS * N_KEYS // 2, tk), lambda i: (0, i))
    return pl.pallas_call(
        _route_kernel,
        out_shape=[s14, s14, s2d, s2d],
        grid=(n // tk,),
        in_specs=[pl.BlockSpec((D_MODEL // 2, tk), lambda i: (0, i)),
                  pl.BlockSpec(wq_t.shape, lambda i: (0, 0)),
                  pl.BlockSpec(k1.shape, lambda i: (0, 0)),
                  pl.BlockSpec(k2.shape, lambda i: (0, 0))],
        out_specs=[spec4, spec4, spec2, spec2],
        compiler_params=_cparams(("arbitrary",)),
        name="peer_route",
    )(h2t, wq_t, k1, k2)


FFN_MXU_TILE = 256
FFN_VPU_TILE = LANES
BF16_ROWS = 16


def _ffn_kernel(h_ref, u_ref, vt_ref, cnt_ref, m1_ref, r2_ref, e2_ref, o_ref, gelu_ref, coef_ref):
    tm = h_ref.shape[1]
    n_bt = N_KEYS // BF16_ROWS

    @pl.when(pl.program_id(1) == 0)
    def _():
        o_ref[...] = jnp.zeros_like(o_ref)

    def row16(ref, h, al, ls):
        return jnp.broadcast_to(ref[h, al:al + 1, ls], (BF16_ROWS, FFN_VPU_TILE)).astype(BF16)

    def mxu_cols(mt):
        return slice(mt * FFN_MXU_TILE, (mt + 1) * FFN_MXU_TILE)

    def build_gates(mt):
        for vt in range(FFN_MXU_TILE // FFN_VPU_TILE):
            lo = mt * FFN_MXU_TILE + vt * FFN_VPU_TILE
            ls = slice(lo, lo + FFN_VPU_TILE)
            for a0 in range(0, EB_FFN // N_KEYS, 2):
                acc = [[jnp.zeros((BF16_ROWS, FFN_VPU_TILE), BF16) for _ in range(n_bt)] for _ in range(2)]
                for h in range(PEER_HEADS):
                    cnts = [row16(cnt_ref, h, a0 + k, ls) for k in range(2)]
                    m1s = [row16(m1_ref, h, a0 + k, ls) for k in range(2)]
                    for bt in range(n_bt):
                        rs = slice((h * N_KEYS + bt * BF16_ROWS) // 2, (h * N_KEYS + (bt + 1) * BF16_ROWS) // 2)
                        rank, e2 = _unpack(r2_ref[rs, ls]), _unpack(e2_ref[rs, ls])
                        for k in range(2):
                            acc[k][bt] = acc[k][bt] + jnp.where(rank < cnts[k], e2, jnp.zeros_like(e2)) * m1s[k]
                for k in range(2):
                    for bt in range(n_bt):
                        es = slice((a0 + k) * N_KEYS + bt * BF16_ROWS, (a0 + k) * N_KEYS + (bt + 1) * BF16_ROWS)
                        coef_ref[es, ls] = acc[k][bt]

    def up_proj(mt):
        x = jnp.dot(_unpack(u_ref[...]), _unpack(h_ref[:, mxu_cols(mt)]), preferred_element_type=F32)
        k0 = (2.0 / jnp.pi) ** 0.5
        hx = 0.5 * x
        gelu_ref[:, mxu_cols(mt)] = hx + hx * jnp.tanh(x * (k0 + (k0 * 0.044715) * (x * x)))

    def down_proj(mt):
        ms = mxu_cols(mt)
        coef_ref[:, ms] = (coef_ref[:, ms].astype(F32) * gelu_ref[:, ms]).astype(BF16)
        o_ref[:, ms] += jnp.dot(_unpack(vt_ref[...]), coef_ref[:, ms], preferred_element_type=F32)

    n_mt = tm // FFN_MXU_TILE
    build_gates(0)
    up_proj(0)
    for mt in range(n_mt):
        if mt + 1 < n_mt:
            build_gates(mt + 1)
            up_proj(mt + 1)
        down_proj(mt)


def _ffn(h2t, u_b, vt_b, cnt, m1, r2, e2):
    n = h2t.shape[1]
    tm, eb = TM_FFN, EB_FFN
    n_exp = 2 * u_b.shape[0]
    spec4 = pl.BlockSpec((None, PEER_HEADS, SUBLANES, tm), lambda i, e: (e, 0, 0, i))
    spec2 = pl.BlockSpec((PEER_HEADS * N_KEYS // 2, tm), lambda i, e: (0, i))
    return pl.pallas_call(
        _ffn_kernel,
        out_shape=jax.ShapeDtypeStruct((D_MODEL, n), F32),
        grid=(n // tm, n_exp // eb),
        in_specs=[pl.BlockSpec((D_MODEL // 2, tm), lambda i, e: (0, i)),
                  pl.BlockSpec((eb // 2, D_MODEL), lambda i, e: (e, 0)),
                  pl.BlockSpec((D_MODEL // 2, eb), lambda i, e: (0, e)),
                  spec4, spec4, spec2, spec2],
        out_specs=pl.BlockSpec((D_MODEL, tm), lambda i, e: (0, i)),
        scratch_shapes=[pltpu.VMEM((eb, tm), F32), pltpu.VMEM((eb, tm), BF16)],
        compiler_params=_cparams(("arbitrary", "arbitrary")),
        name="peer_ffn",
    )(h2t, u_b, vt_b, cnt, m1, r2, e2)


def _final_kernel(p_ref, x_ref, g2_ref, nf_ref, y_ref):
    x2 = x_ref[...] + g2_ref[...] * p_ref[...].T
    ms = jnp.mean(x2 * x2, axis=-1, keepdims=True)
    y_ref[...] = x2 * lax.rsqrt(ms + NORM_EPS) * nf_ref[...]


def _final(peer_t, col_off, x1, mod_spec, g2, nf):
    n = x1.shape[0]
    tm = TM_FINAL
    off = col_off // tm
    return pl.pallas_call(
        _final_kernel,
        out_shape=jax.ShapeDtypeStruct((n, D_MODEL), F32),
        grid=(n // tm,),
        in_specs=[pl.BlockSpec((D_MODEL, tm), lambda i: (0, i + off)),
                  pl.BlockSpec((tm, D_MODEL), lambda i: (i, 0)),
                  mod_spec,
                  pl.BlockSpec((1, D_MODEL), lambda i: (0, 0))],
        out_specs=pl.BlockSpec((tm, D_MODEL), lambda i: (i, 0)),
        compiler_params=_cparams(("arbitrary",)),
        name="final",
    )(peer_t, x1, g2, nf)


def _rope_tables(pos):
    half = HEAD_DIM // 2
    inv = ROPE_THETA ** (-jnp.arange(half, dtype=F32) / half)
    ang = pos.astype(F32)[:, None] * inv[None, :]
    cos, sin = jnp.cos(ang), jnp.sin(ang)
    zero = jnp.zeros_like(sin)
    reps = LANES // HEAD_DIM
    cos_t = jnp.tile(jnp.concatenate([cos, cos], axis=1), (1, reps))
    sa_t = jnp.tile(jnp.concatenate([-sin, zero], axis=1), (1, reps))
    sb_t = jnp.tile(jnp.concatenate([zero, sin], axis=1), (1, reps))
    return cos_t, sa_t, sb_t


def kernel(x_prompt, x_sample, cache_kv0, cache_kv1, cache_kv2, state_conv, c_prompt, c_sample,
           norm1_g, norm2_g, norm_f_g, w_ada, b_ada, w_in, conv_w, w_attn_o, w_conv_o, w_o,
           w_query, sub_keys, expert_u, expert_v):
    batch, seq, _ = x_prompt.shape
    n_seq, s_len, _ = x_sample.shape
    depth = w_in.shape[0]
    assert depth == 1
    assert tuple(c.shape[2] for c in (cache_kv0, cache_kv1, cache_kv2)) == WINDOWS
    assert s_len == SUBLANES and seq % (ATTN_STEP * DILATIONS[2]) == 0
    n_p, n_s = batch * seq, n_seq * s_len
    l = 0

    w_in_b = _pack_weight(w_in[l])
    wa_b, wc_b, wo_b = _pack_weight(w_attn_o[l]), _pack_weight(w_conv_o[l]), _pack_weight(w_o[l])
    wq_t = _pack_weight(w_query[l], transpose=True)
    k1_b, k2_b = sub_keys[l, 0].astype(BF16), sub_keys[l, 1].astype(BF16)
    u_b = _pack_weight(expert_u[l])
    vt_b = _pack_weight(expert_v[l], transpose=True)
    n1, n2, nf = norm1_g[l][None, :], norm2_g[l][None, :], norm_f_g[None, :]

    n_c = batch + n_seq
    c_all = jnp.concatenate([c_prompt, c_sample, jnp.zeros((-n_c % SUBLANES, D_MODEL), F32)], axis=0)
    mod = _ada(c_all, w_ada[l], b_ada[l])
    mod_p = [m[:, None, :] for m in jnp.split(mod[:batch], 6, axis=-1)]
    mod_s = jnp.split(jnp.repeat(mod[batch:n_c], s_len, axis=0), 6, axis=-1)

    def seq_spec(tm):
        return pl.BlockSpec((None, 1, D_MODEL), lambda i: (i // (seq // tm), 0, 0))

    def tok_spec(tm):
        return pl.BlockSpec((tm, D_MODEL), lambda i: (i, 0))

    tabs_p = _rope_tables(jnp.arange(seq))
    tabs_s = [jnp.tile(t, (n_seq, 1)) for t in _rope_tables(PAST_LEN + jnp.arange(s_len))]
    tab_p_spec = pl.BlockSpec((TM_PROJ, LANES), lambda i: (i % (seq // TM_PROJ), 0))
    tab_s_spec = pl.BlockSpec((TM_PROJ, LANES), lambda i: (i, 0))

    xp = x_prompt.reshape(n_p, D_MODEL)
    xs = x_sample.reshape(n_s, D_MODEL)
    *qkv_p, bgp, up, gap, gcp = _inproj(xp, seq_spec(TM_PROJ), (mod_p[1], mod_p[0]), tab_p_spec, tabs_p, n1, w_in_b,
                                        residue_seq=seq)
    qp, kp, vp = qkv_p[0:N_GROUPS], qkv_p[N_GROUPS:2 * N_GROUPS], qkv_p[2 * N_GROUPS:]
    qs, ks, vs, bgs, us, gas, gcs = _inproj(xs, tok_spec(TM_PROJ), (mod_s[1], mod_s[0]), tab_s_spec, tabs_s, n1, w_in_b)

    attn_p = [_pattn(qp[g], kp[g], vp[g], g) for g in range(N_GROUPS)]
    caches = (cache_kv0[l], cache_kv1[l], cache_kv2[l])
    attn_s = _sattn(qs, ks, vs, caches, n_seq, s_len)

    x1p, h2tp = _merge_prompt(attn_p, bgp, up, conv_w[l], gap, gcp, xp, seq_spec(TM_MERGE),
                              mod_p[2], n2, mod_p[4], mod_p[3], wa_b, wc_b, wo_b, seq)
    st = state_conv[l]
    p1 = jnp.pad(st[:, 1:2], ((0, 0), (0, s_len - 1), (0, 0))).reshape(n_s, CONV_WIDTH)
    p2 = jnp.pad(st, ((0, 0), (0, s_len - 2), (0, 0))).reshape(n_s, CONV_WIDTH)
    x1s, h2ts = _merge_sample(attn_s, bgs, us, p1, p2, conv_w[l], gas, gcs, xs, tok_spec(TM_MERGE),
                              mod_s[2], n2, mod_s[4], mod_s[3], wa_b, wc_b, wo_b, s_len)

    h2t = jnp.concatenate([h2tp, h2ts], axis=1)
    cnt, m1, r2, e2 = _route(h2t, wq_t, k1_b, k2_b)
    peer_t = _ffn(h2t, u_b, vt_b, cnt, m1, r2, e2)

    yp = _final(peer_t, 0, x1p, seq_spec(TM_FINAL), mod_p[5], nf).reshape(batch, seq, D_MODEL)
    ys = _final(peer_t, n_p, x1s, tok_spec(TM_FINAL), mod_s[5], nf).reshape(n_seq, s_len, D_MODEL)

    def prompt_tail(a, g):
        d, n_rows = a.shape[1], a.shape[2]
        t = jnp.swapaxes(a[:, :, n_rows - N_BACK:, :], 1, 2)
        return t.reshape(batch, N_BACK * d, HEADS_PER_GROUP, HEAD_DIM)

    kv_p, kv_s = [], []
    for g in range(N_GROUPS):
        assert WINDOWS[g] == N_BACK * DILATIONS[g] <= seq
        gs = slice(g * GROUP_WIDTH, (g + 1) * GROUP_WIDTH)
        kv_p.append(jnp.stack([prompt_tail(kp[g], g), prompt_tail(vp[g], g)], axis=2)[None])
        k4 = ks.reshape(n_seq, s_len, ATTN_WIDTH)[:, :, gs].reshape(n_seq, s_len, HEADS_PER_GROUP, HEAD_DIM)
        v4 = vs.reshape(n_seq, s_len, ATTN_WIDTH)[:, :, gs].reshape(n_seq, s_len, HEADS_PER_GROUP, HEAD_DIM)
        kv_s.append(jnp.stack([k4, v4], axis=2)[None])
    conv_p = up.reshape(batch, seq, CONV_WIDTH)[:, seq - (CONV_K - 1):][None]
    conv_s = jnp.concatenate([st, us.reshape(n_seq, s_len, CONV_WIDTH)], axis=1)[:, s_len:][None]
    return (yp, ys, kv_p[0], kv_p[1], kv_p[2], conv_p, kv_s[0], kv_s[1], kv_s[2], conv_s)
```

```python
import functools

import jax
import jax.numpy as jnp
from jax import lax
from jax.experimental import pallas as pl
from jax.experimental.pallas import tpu as pltpu

F32 = jnp.float32
BF16 = jnp.bfloat16
U32 = jnp.uint32

D_MODEL = 1024
HEAD_DIM = 64
HEADS_PER_GROUP = 4
WINDOWS = (128, 512, 2048)
DILATIONS = (1, 4, 16)
N_GROUPS = 3
GROUP_WIDTH = HEADS_PER_GROUP * HEAD_DIM
ATTN_WIDTH = N_GROUPS * GROUP_WIDTH
CONV_WIDTH = 768
CONV_K = 3
IN_WIDTH = 3 * ATTN_WIDTH + 3 * CONV_WIDTH + 2 * D_MODEL
ATTN_SCALE = HEAD_DIM ** -0.5
ROPE_THETA = 10000.0
N_KEYS = 128
PEER_HEADS = 8
PEER_TOPK = 16
NORM_EPS = 1e-6
N_BACK = 128
PAST_LEN = 8192

LANES = 128
SUBLANES = 8
VMEM_LIMIT = 52 * 1024 * 1024

TM_PROJ = 256
TM_MERGE = 256
TM_FINAL = 256
TK_ROUTE = 256
TM_FFN = 1024
EB_FFN = 1024
ATTN_BLK = 128
ATTN_STEP = 512


def _unpack(packed):
    return pltpu.bitcast(packed, BF16)


def _pack_kernel(w_ref, o_ref, *, transpose):
    w = w_ref[...]
    if transpose:
        w = w.T
    o_ref[...] = pltpu.bitcast(w.astype(BF16), U32)


def _pack_weight(w, transpose=False):
    r, c = w.shape
    tr = next(t for t in (1024, 512, 256, 128) if r % t == 0)
    tc = next(t for t in (1024, 512, 256, 128) if c % t == 0)
    if transpose:
        out_shape, out_spec = (c // 2, r), pl.BlockSpec((tc // 2, tr), lambda i, j: (j, i))
    else:
        out_shape, out_spec = (r // 2, c), pl.BlockSpec((tr // 2, tc), lambda i, j: (i, j))
    return pl.pallas_call(
        functools.partial(_pack_kernel, transpose=transpose),
        out_shape=jax.ShapeDtypeStruct(out_shape, U32),
        grid=(r // tr, c // tc),
        in_specs=[pl.BlockSpec((tr, tc), lambda i, j: (i, j))],
        out_specs=out_spec,
        compiler_params=_cparams(("arbitrary", "arbitrary")),
        name="pack_weight",
    )(w)


def _cparams(sem, flags=None):
    return pltpu.CompilerParams(dimension_semantics=sem, vmem_limit_bytes=VMEM_LIMIT, flags=flags)


def _ada_kernel(c_ref, w_ref, b_ref, o_ref):
    c = c_ref[...]
    a = (c * jax.nn.sigmoid(c)).astype(BF16)
    o_ref[...] = jnp.dot(a, w_ref[...].astype(BF16), preferred_element_type=F32) + b_ref[...]


def _ada(c_all, w_ada, b_ada):
    rows = c_all.shape[0]
    n_out = w_ada.shape[1]
    tn = 1024
    return pl.pallas_call(
        _ada_kernel,
        out_shape=jax.ShapeDtypeStruct((rows, n_out), F32),
        grid=(n_out // tn,),
        in_specs=[pl.BlockSpec((rows, D_MODEL), lambda j: (0, 0)),
                  pl.BlockSpec((D_MODEL, tn), lambda j: (0, j)),
                  pl.BlockSpec((1, tn), lambda j: (0, j))],
        out_specs=pl.BlockSpec((rows, tn), lambda j: (0, j)),
        compiler_params=_cparams(("arbitrary",)),
        name="adaln",
    )(c_all, w_ada, b_ada.reshape(1, n_out))


def _rmsnorm_mod(x, g, scale, shift):
    ms = jnp.mean(x * x, axis=-1, keepdims=True)
    y = x * lax.rsqrt(ms + NORM_EPS) * g
    return y * (1.0 + scale) + shift


def _inproj_kernel(x_ref, sc_ref, sh_ref, n1_ref, cos_ref, sa_ref, sb_ref, w_ref, *refs, by_residue):
    n_qkv = 3 * N_GROUPS if by_residue else 3
    bg_ref, u_ref, ga_ref, gc_ref, zbuf = refs[n_qkv:]
    n_chunks, tm = zbuf.shape[0], zbuf.shape[1]
    per_group = GROUP_WIDTH // LANES
    h = _rmsnorm_mod(x_ref[...], n1_ref[...], sc_ref[...], sh_ref[...]).astype(BF16)

    def proj(lo, hi):
        return jnp.dot(h, _unpack(w_ref[:, lo:hi]), preferred_element_type=F32)

    cos, sa, sb = cos_ref[...], sa_ref[...], sb_ref[...]

    def stage(z, rope, mult):
        for c in range(n_chunks):
            zc = z[:, c * LANES:(c + 1) * LANES]
            if rope:
                zc = zc * cos + pltpu.roll(zc, LANES - 32, 1) * sa + pltpu.roll(zc, 32, 1) * sb
            zbuf[c] = zc * mult if mult != 1.0 else zc

    def emit(which):
        for c in range(n_chunks):
            if not by_residue:
                refs[which][:, c * LANES:(c + 1) * LANES] = zbuf[c].astype(refs[which].dtype)
                continue
            g, cl = c // per_group, c % per_group
            ref, d = refs[which * N_GROUPS + g], DILATIONS[g]
            for r in range(d):
                ref[r, :, cl * LANES:(cl + 1) * LANES] = zbuf[c, pl.ds(r, tm // d, stride=d), :].astype(ref.dtype)

    o = 0
    stage(proj(o, o + ATTN_WIDTH), True, ATTN_SCALE)
    emit(0)
    o += ATTN_WIDTH
    stage(proj(o, o + ATTN_WIDTH), True, 1.0)
    emit(1)
    o += ATTN_WIDTH
    stage(proj(o, o + ATTN_WIDTH), False, 1.0)
    emit(2)
    o += ATTN_WIDTH
    bg_ref[...] = proj(o, o + CONV_WIDTH)
    o += CONV_WIDTH
    cg = proj(o, o + CONV_WIDTH)
    o += CONV_WIDTH
    u_ref[...] = cg * proj(o, o + CONV_WIDTH)
    o += CONV_WIDTH
    ga_ref[...] = proj(o, o + D_MODEL)
    o += D_MODEL
    gc_ref[...] = proj(o, o + D_MODEL)


def _inproj(x2d, mod_specs, mods, tab_spec, tabs, n1, w_in_p, residue_seq=None):
    n = x2d.shape[0]
    tm = TM_PROJ
    row = lambda w: pl.BlockSpec((tm, w), lambda i: (i, 0))
    const = lambda s: pl.BlockSpec(s, lambda i: (0, 0))
    if residue_seq is None:
        qkv_shapes = [jax.ShapeDtypeStruct((n, ATTN_WIDTH), dt) for dt in (BF16, F32, F32)]
        qkv_specs = [row(ATTN_WIDTH)] * 3
    else:
        bps = residue_seq // tm
        qkv_shapes, qkv_specs = [], []
        for dt in (BF16, F32, F32):
            for d in DILATIONS:
                qkv_shapes.append(jax.ShapeDtypeStruct((n // residue_seq, d, residue_seq // d, GROUP_WIDTH), dt))
                qkv_specs.append(pl.BlockSpec((None, d, tm // d, GROUP_WIDTH), lambda i: (i // bps, 0, i % bps, 0)))
    outs = qkv_shapes + [jax.ShapeDtypeStruct((n, CONV_WIDTH), F32),
                         jax.ShapeDtypeStruct((n, CONV_WIDTH), F32),
                         jax.ShapeDtypeStruct((n, D_MODEL), F32),
                         jax.ShapeDtypeStruct((n, D_MODEL), F32)]
    return pl.pallas_call(
        functools.partial(_inproj_kernel, by_residue=residue_seq is not None),
        out_shape=outs,
        grid=(n // tm,),
        in_specs=[row(D_MODEL), mod_specs, mod_specs, const((1, D_MODEL)),
                  tab_spec, tab_spec, tab_spec, const(w_in_p.shape)],
        out_specs=qkv_specs + [row(CONV_WIDTH)] * 2 + [row(D_MODEL)] * 2,
        scratch_shapes=[pltpu.VMEM((ATTN_WIDTH // LANES, tm, LANES), F32)],
        compiler_params=_cparams(("arbitrary",)),
        name="inproj",
    )(x2d, mods[0], mods[1], n1, tabs[0], tabs[1], tabs[2], w_in_p)


def _pattn_kernel(q_ref, kp_ref, kc_ref, vp_ref, vc_ref, o_ref, l_ref):
    nb = pl.program_id(2)
    k = jnp.concatenate([kp_ref[...], kc_ref[...]], axis=0).astype(BF16)
    v = jnp.concatenate([vp_ref[...], vc_ref[...]], axis=0).astype(BF16)
    qi = lax.broadcasted_iota(jnp.int32, (ATTN_BLK, 2 * ATTN_BLK), 0)
    ki = lax.broadcasted_iota(jnp.int32, (ATTN_BLK, 2 * ATTN_BLK), 1)
    dist = qi + ATTN_BLK - ki
    band = (dist >= 0) & (dist <= N_BACK)
    band0 = band & (ki >= jnp.where(nb > 0, 0, ATTN_BLK))
    for sb in range(ATTN_STEP // ATTN_BLK):
        rows = slice(sb * ATTN_BLK, (sb + 1) * ATTN_BLK)
        keys = slice(sb * ATTN_BLK, (sb + 2) * ATTN_BLK)
        q = q_ref[rows, :]
        valid = band0 if sb == 0 else band
        for h in range(HEADS_PER_GROUP):
            sl = slice(h * HEAD_DIM, (h + 1) * HEAD_DIM)
            s = lax.dot_general(q[:, sl], k[keys, sl], (((1,), (1,)), ((), ())), preferred_element_type=F32)
            s = jnp.where(valid, s, -jnp.inf)
            m = jnp.max(s, axis=-1, keepdims=True)
            p = jnp.exp(s - m)
            den = jnp.sum(p, axis=-1, keepdims=True)
            o = jnp.dot(p.astype(BF16), v[keys, sl], preferred_element_type=F32) / den
            o_ref[rows, sl] = o
            l_ref[rows, sl] = jnp.broadcast_to(m + jnp.log(den), (ATTN_BLK, HEAD_DIM))


def _pattn(q, k, v, g):
    batch, d, n, _ = q.shape
    per_blk = ATTN_STEP // ATTN_BLK
    cur = pl.BlockSpec((None, None, ATTN_STEP, GROUP_WIDTH), lambda b, r, nb: (b, r, nb, 0))
    prev = pl.BlockSpec((None, None, ATTN_BLK, GROUP_WIDTH),
                        lambda b, r, nb: (b, r, jnp.maximum(nb * per_blk - 1, 0), 0))
    return pl.pallas_call(
        _pattn_kernel,
        out_shape=[jax.ShapeDtypeStruct((batch, d, n, GROUP_WIDTH), F32)] * 2,
        grid=(batch, d, n // ATTN_STEP),
        in_specs=[cur, prev, cur, prev, cur],
        out_specs=[cur, cur],
        compiler_params=_cparams(("arbitrary",) * 3),
        name=f"pattn{g}",
    )(q, k, k, v, v)


def _sattn_kernel(q_ref, k_ref, v_ref, c0_ref, c1_ref, c2_ref, o_ref, *, s_len):
    q = q_ref[...].astype(F32)
    kn = k_ref[...]
    vn = v_ref[...]
    rows = HEADS_PER_GROUP * s_len
    lane_head = lax.broadcasted_iota(jnp.int32, (s_len, GROUP_WIDTH), 1) // HEAD_DIM
    pad = jnp.zeros((LANES - s_len, GROUP_WIDTH), F32)
    nt_dims = (((1,), (1,)), ((), ()))
    ms, ls, os_ = [], [], []
    for g, cref in enumerate((c0_ref, c1_ref, c2_ref)):
        d, win = DILATIONS[g], WINDOWS[g]
        gs = slice(g * GROUP_WIDTH, (g + 1) * GROUP_WIDTH)
        qg = q[:, gs]
        qexp = jnp.concatenate(
            [jnp.where(lane_head == h, qg, 0.0) for h in range(HEADS_PER_GROUP)], axis=0).astype(BF16)
        s_c = jnp.dot(qexp, cref[0].astype(BF16), preferred_element_type=F32)
        s_n = lax.dot_general(qexp, jnp.concatenate([kn[:, gs], pad], axis=0).astype(BF16), nt_dims,
                              preferred_element_type=F32)
        back_c = (win + lax.broadcasted_iota(jnp.int32, (rows, win), 0) % s_len
                  - lax.broadcasted_iota(jnp.int32, (rows, win), 1))
        col_n = lax.broadcasted_iota(jnp.int32, (rows, LANES), 1)
        back_n = lax.broadcasted_iota(jnp.int32, (rows, LANES), 0) % s_len - col_n
        s_c = jnp.where((back_c <= win) & ((back_c & (d - 1)) == 0), s_c, -jnp.inf)
        s_n = jnp.where((back_n >= 0) & ((back_n & (d - 1)) == 0) & (col_n < s_len), s_n, -jnp.inf)
        m = jnp.maximum(jnp.max(s_c, axis=-1, keepdims=True), jnp.max(s_n, axis=-1, keepdims=True))
        p_c = jnp.exp(s_c - m)
        p_n = jnp.exp(s_n - m)
        ls.append(jnp.sum(p_c, axis=-1, keepdims=True) + jnp.sum(p_n, axis=-1, keepdims=True))
        ms.append(m)
        o = lax.dot_general(p_c.astype(BF16), cref[1].astype(BF16), nt_dims, preferred_element_type=F32)
        o = o + jnp.dot(p_n.astype(BF16), jnp.concatenate([vn[:, gs], pad], axis=0).astype(BF16),
                        preferred_element_type=F32)
        os_.append(o)
    mm = jnp.maximum(jnp.maximum(ms[0], ms[1]), ms[2])
    num = jnp.zeros((rows, GROUP_WIDTH), F32)
    den = jnp.zeros((rows, 1), F32)
    for g in range(N_GROUPS):
        w = jnp.exp(ms[g] - mm)
        num = num + w * os_[g]
        den = den + w * ls[g]
    full = num / den
    out = jnp.zeros((s_len, GROUP_WIDTH), F32)
    for h in range(HEADS_PER_GROUP):
        out = out + jnp.where(lane_head == h, full[h * s_len:(h + 1) * s_len], 0.0)
    o_ref[...] = out


def _sattn(q, k, v, caches, n_seq, s_len):
    q3 = q.reshape(n_seq, s_len, ATTN_WIDTH)
    k3 = k.reshape(n_seq, s_len, ATTN_WIDTH)
    v3 = v.reshape(n_seq, s_len, ATTN_WIDTH)
    cts = [jnp.transpose(c, (0, 2, 3, 4, 1)).reshape(n_seq, 2, GROUP_WIDTH, c.shape[1]) for c in caches]
    tok = pl.BlockSpec((None, s_len, ATTN_WIDTH), lambda b: (b, 0, 0))
    out = pl.pallas_call(
        functools.partial(_sattn_kernel, s_len=s_len),
        out_shape=jax.ShapeDtypeStruct((n_seq, s_len, GROUP_WIDTH), F32),
        grid=(n_seq,),
        in_specs=[tok, tok, tok] + [pl.BlockSpec((None, 2, GROUP_WIDTH, w), lambda b: (b, 0, 0, 0)) for w in WINDOWS],
        out_specs=pl.BlockSpec((None, s_len, GROUP_WIDTH), lambda b: (b, 0, 0)),
        compiler_params=_cparams(("arbitrary",)),
        name="sattn",
    )(q3, k3, v3, *cts)
    return out.reshape(n_seq * s_len, GROUP_WIDTH)


def _merge_tail(o_attn, bg, yc, ga_ref, gc_ref, x_ref, g1_ref, n2_ref, sc2_ref, sh2_ref,
                wa_ref, wc_ref, wo_ref, x1_ref, h2t_ref):
    a_out = jnp.dot(o_attn.astype(BF16), _unpack(wa_ref[...]), preferred_element_type=F32)
    c_out = jnp.dot((bg * yc).astype(BF16), _unpack(wc_ref[...]), preferred_element_type=F32)
    mix = jax.nn.sigmoid(ga_ref[...]) * a_out + jax.nn.sigmoid(gc_ref[...]) * c_out
    mo = jnp.dot(mix.astype(BF16), _unpack(wo_ref[...]), preferred_element_type=F32)
    x1 = x_ref[...] + g1_ref[...] * mo
    x1_ref[...] = x1
    h2 = _rmsnorm_mod(x1, n2_ref[...], sc2_ref[...], sh2_ref[...])
    h2t_ref[...] = pltpu.bitcast(h2.T.astype(BF16), U32)


def _conv3(u, um1, um2, cw_ref):
    return cw_ref[0:1, :] * um2 + cw_ref[1:2, :] * um1 + cw_ref[2:3, :] * u


def _merge_prompt_kernel(o0_ref, o1_ref, o2_ref, l0_ref, l1_ref, l2_ref, bg_ref, u_ref, uh_ref, cw_ref,
                         ga_ref, gc_ref, x_ref, g1_ref, n2_ref, sc2_ref, sh2_ref, wa_ref, wc_ref, wo_ref,
                         x1_ref, h2t_ref, ubuf, *rowbufs, blocks_per_seq):
    tm = u_ref.shape[0]
    first = (pl.program_id(0) % blocks_per_seq) == 0

    def by_position(ref, buf):
        d, rows = ref.shape[0], ref.shape[1]
        if d == 1:
            return ref[0]
        for r in range(d):
            for c in range(GROUP_WIDTH // LANES):
                buf[c, pl.ds(r, rows, stride=d), :] = ref[r, :, c * LANES:(c + 1) * LANES]
        return jnp.concatenate([buf[c] for c in range(GROUP_WIDTH // LANES)], axis=1)

    o0, l0 = by_position(o0_ref, None), by_position(l0_ref, None)
    o1, l1 = by_position(o1_ref, rowbufs[0]), by_position(l1_ref, rowbufs[1])
    o2, l2 = by_position(o2_ref, rowbufs[2]), by_position(l2_ref, rowbufs[3])
    mm = jnp.maximum(jnp.maximum(l0, l1), l2)
    e0, e1, e2 = jnp.exp(l0 - mm), jnp.exp(l1 - mm), jnp.exp(l2 - mm)
    o_attn = (e0 * o0 + e1 * o1 + e2 * o2) / (e0 + e1 + e2)
    u = u_ref[...]
    ubuf[0:SUBLANES, :] = jnp.where(first, 0.0, uh_ref[...])
    ubuf[SUBLANES:SUBLANES + tm, :] = u
    yc = _conv3(u, ubuf[SUBLANES - 1:SUBLANES - 1 + tm, :], ubuf[SUBLANES - 2:SUBLANES - 2 + tm, :], cw_ref)
    _merge_tail(o_attn, bg_ref[...], yc, ga_ref, gc_ref, x_ref, g1_ref, n2_ref, sc2_ref, sh2_ref,
                wa_ref, wc_ref, wo_ref, x1_ref, h2t_ref)


def _merge_sample_kernel(oa_ref, bg_ref, u_ref, p1_ref, p2_ref, cw_ref,
                         ga_ref, gc_ref, x_ref, g1_ref, n2_ref, sc2_ref, sh2_ref, wa_ref, wc_ref, wo_ref,
                         x1_ref, h2t_ref, ubuf, *, s_len):
    tm = u_ref.shape[0]
    u = u_ref[...]
    ubuf[0:SUBLANES, :] = jnp.zeros((SUBLANES, CONV_WIDTH), F32)
    ubuf[SUBLANES:SUBLANES + tm, :] = u
    t = lax.broadcasted_iota(jnp.int32, (tm, CONV_WIDTH), 0) % s_len
    um1 = jnp.where(t < 1, p1_ref[...], ubuf[SUBLANES - 1:SUBLANES - 1 + tm, :])
    um2 = jnp.where(t < 2, p2_ref[...], ubuf[SUBLANES - 2:SUBLANES - 2 + tm, :])
    yc = _conv3(u, um1, um2, cw_ref)
    _merge_tail(oa_ref[...], bg_ref[...], yc, ga_ref, gc_ref, x_ref, g1_ref, n2_ref, sc2_ref, sh2_ref,
                wa_ref, wc_ref, wo_ref, x1_ref, h2t_ref)


def _merge_common_specs(tm, mod_spec):
    row = lambda w: pl.BlockSpec((tm, w), lambda i: (i, 0))
    const = lambda s: pl.BlockSpec(s, lambda i: (0, 0))
    ins = [row(D_MODEL), row(D_MODEL), row(D_MODEL), mod_spec, const((1, D_MODEL)), mod_spec, mod_spec,
           const((GROUP_WIDTH // 2, D_MODEL)), const((CONV_WIDTH // 2, D_MODEL)), const((D_MODEL // 2, D_MODEL))]
    outs = [row(D_MODEL), pl.BlockSpec((D_MODEL // 2, tm), lambda i: (0, i))]
    return ins, outs


def _merge_prompt(attn, bg, u, conv_w, ga, gc, x2d, mod_spec, g1, n2, sc2, sh2, wa, wc, wo, seq):
    n = x2d.shape[0]
    tm = TM_MERGE
    row = lambda w: pl.BlockSpec((tm, w), lambda i: (i, 0))
    halo = pl.BlockSpec((SUBLANES, CONV_WIDTH), lambda i: (jnp.maximum(i * (tm // SUBLANES) - 1, 0), 0))
    ins, outs = _merge_common_specs(tm, mod_spec)
    (o0, l0), (o1, l1), (o2, l2) = attn
    bps = seq // tm
    grp = [pl.BlockSpec((None, d, tm // d, GROUP_WIDTH), lambda i: (i // bps, 0, i % bps, 0)) for d in DILATIONS]
    return pl.pallas_call(
        functools.partial(_merge_prompt_kernel, blocks_per_seq=bps),
        out_shape=[jax.ShapeDtypeStruct((n, D_MODEL), F32), jax.ShapeDtypeStruct((D_MODEL // 2, n), U32)],
        grid=(n // tm,),
        in_specs=grp + grp + [row(CONV_WIDTH), row(CONV_WIDTH), halo,
                              pl.BlockSpec((CONV_K, CONV_WIDTH), lambda i: (0, 0))] + ins,
        out_specs=outs,
        scratch_shapes=[pltpu.VMEM((SUBLANES + tm, CONV_WIDTH), F32)] + [pltpu.VMEM((GROUP_WIDTH // LANES, tm, LANES), F32)] * 4,
        compiler_params=_cparams(("arbitrary",)),
        name="merge_prompt",
    )(o0, o1, o2, l0, l1, l2, bg, u, u, conv_w, ga, gc, x2d, g1, n2, sc2, sh2, wa, wc, wo)


def _merge_sample(o_attn, bg, u, p1, p2, conv_w, ga, gc, x2d, mod_spec, g1, n2, sc2, sh2, wa, wc, wo, s_len):
    n = x2d.shape[0]
    tm = TM_MERGE
    row = lambda w: pl.BlockSpec((tm, w), lambda i: (i, 0))
    ins, outs = _merge_common_specs(tm, mod_spec)
    return pl.pallas_call(
        functools.partial(_merge_sample_kernel, s_len=s_len),
        out_shape=[jax.ShapeDtypeStruct((n, D_MODEL), F32), jax.ShapeDtypeStruct((D_MODEL // 2, n), U32)],
        grid=(n // tm,),
        in_specs=[row(GROUP_WIDTH)] + [row(CONV_WIDTH)] * 4 + [pl.BlockSpec((CONV_K, CONV_WIDTH), lambda i: (0, 0))] + ins,
        out_specs=outs,
        scratch_shapes=[pltpu.VMEM((SUBLANES + tm, CONV_WIDTH), F32)],
        compiler_params=_cparams(("arbitrary",)),
        name="merge_sample",
    )(o_attn, bg, u, p1, p2, conv_w, ga, gc, x2d, g1, n2, sc2, sh2, wa, wc, wo)


def _oddeven_merge_sort_pairs(n):
    pairs = []
    p = 1
    while p < n:
        k = p
        while k >= 1:
            for j in range(k % p, n - k, 2 * k):
                for i in range(min(k, n - j - k)):
                    if (i + j) // (2 * p) == (i + j + k) // (2 * p):
                        pairs.append((i + j, i + j + k))
            k //= 2
        p *= 2
    return pairs


_SORT16 = _oddeven_merge_sort_pairs(PEER_TOPK)


def _cmpx(x, a, b):
    hi, lo = jnp.maximum(x[a], x[b]), jnp.minimum(x[a], x[b])
    x[a], x[b] = hi, lo


def _bitonic_clean(x):
    stride = PEER_TOPK // 2
    while stride >= 1:
        for i in range(PEER_TOPK):
            if i & stride == 0:
                _cmpx(x, i, i + stride)
        stride //= 2
    return x


def _merge_top16(a, b):
    c = []
    for j in range(PEER_TOPK):
        jb = PEER_TOPK - 1 - j
        c.append(jnp.maximum(a[j], b[jb]) if jb < len(b) else a[j])
    return _bitonic_clean(c)


def _top16_over_keys(s):
    x = [s[j * SUBLANES:(j + 1) * SUBLANES, :] for j in range(N_KEYS // SUBLANES)]
    for a, b in _SORT16:
        _cmpx(x, a, b)
    shift = SUBLANES // 2
    while shift >= 1:
        x = _merge_top16(x, [pltpu.roll(v, shift, 0) for v in x])
        shift //= 2
    return x


def _route_kernel(h_ref, wq_ref, k1_ref, k2_ref, cnt_ref, m1_ref, r2_ref, e2_ref):
    tk = h_ref.shape[1]
    qt = jnp.dot(_unpack(wq_ref[...]), _unpack(h_ref[...]), preferred_element_type=F32).astype(BF16)
    k1, k2 = k1_ref[...], k2_ref[...]
    sub = lax.broadcasted_iota(jnp.int32, (SUBLANES, tk), 0)
    half = N_KEYS
    s1s, s2s = [], []
    v1 = [jnp.zeros((SUBLANES, tk), F32) for _ in range(PEER_TOPK)]
    v2 = [jnp.zeros((SUBLANES, tk), F32) for _ in range(PEER_TOPK)]
    for h in range(PEER_HEADS):
        base = h * 2 * half
        s1 = jnp.dot(k1, qt[base:base + half], preferred_element_type=F32)
        s2 = jnp.dot(k2, qt[base + half:base + 2 * half], preferred_element_type=F32)
        s1s.append(s1)
        s2s.append(s2)
        t1 = _top16_over_keys(s1)
        t2 = _top16_over_keys(s2)
        for j in range(PEER_TOPK):
            v1[j] = jnp.where(sub == h, t1[j], v1[j])
            v2[j] = jnp.where(sub == h, t2[j], v2[j])
    psum = {}
    for a in range(PEER_TOPK):
        for b in range(PEER_TOPK // (a + 1)):
            psum[a, b] = v1[a] + v2[b]
    lists = [[psum[a, b] for b in range(PEER_TOPK // (a + 1))] for a in range(PEER_TOPK // 2)]
    lists.append([psum[a, 0] for a in range(PEER_TOPK // 2, PEER_TOPK)])
    top = lists[0]
    for other in lists[1:]:
        top = _merge_top16(top, other)
    tau = top[PEER_TOPK - 1]
    z = jnp.ones((SUBLANES, tk), F32)
    for j in range(1, PEER_TOPK):
        z = z + jnp.exp(top[j] - top[0])
    rz = 1.0 / z
    x1 = []
    for b in range(PEER_TOPK):
        x = jnp.full((SUBLANES, tk), jnp.inf, F32)
        for a in range(PEER_TOPK // (b + 1)):
            x = jnp.where(psum[a, b] >= tau, v1[a], x)
        x1.append(x)
    for h in range(PEER_HEADS):
        s1, s2 = s1s[h], s2s[h]
        row = lambda v: v[h:h + 1, :]
        cnt = jnp.zeros_like(s1)
        rank = jnp.full_like(s2, float(PEER_TOPK))
        for j in range(PEER_TOPK):
            cnt = jnp.where(s1 >= row(x1[j]), float(j + 1), cnt)
        for j in range(PEER_TOPK - 1, -1, -1):
            rank = jnp.where(s2 >= row(v2[j]), float(j), rank)
        m1 = jnp.exp(s1 - row(v1[0])) * row(rz)
        cnt_ref[:, h, :, :] = cnt.reshape(N_KEYS // SUBLANES, SUBLANES, tk)
        m1_ref[:, h, :, :] = m1.reshape(N_KEYS // SUBLANES, SUBLANES, tk)
        half_rows = slice(h * N_KEYS // 2, (h + 1) * N_KEYS // 2)
        r2_ref[half_rows, :] = pltpu.bitcast(rank.astype(BF16), U32)
        e2_ref[half_rows, :] = pltpu.bitcast(jnp.exp(s2 - row(v2[0])).astype(BF16), U32)


def _route(h2t, wq_t, k1, k2):
    n = h2t.shape[1]
    tk = TK_ROUTE
    a8 = N_KEYS // SUBLANES
    s14 = jax.ShapeDtypeStruct((a8, PEER_HEADS, SUBLANES, n), F32)
    s2d = jax.ShapeDtypeStruct((PEER_HEADS * N_KEYS // 2, n), U32)
    spec4 = pl.BlockSpec((a8, PEER_HEADS, SUBLANES, tk), lambda i: (0, 0, 0, i))
    spec2 = pl.BlockSpec((PEER_HEADS * N_KEYS // 2, tk), lambda i: (0, i))
    return pl.pallas_call(
        _route_kernel,
        out_shape=[s14, s14, s2d, s2d],
        grid=(n // tk,),
        in_specs=[pl.BlockSpec((D_MODEL // 2, tk), lambda i: (0, i)),
                  pl.BlockSpec(wq_t.shape, lambda i: (0, 0)),
                  pl.BlockSpec(k1.shape, lambda i: (0, 0)),
                  pl.BlockSpec(k2.shape, lambda i: (0, 0))],
        out_specs=[spec4, spec4, spec2, spec2],
        compiler_params=_cparams(("arbitrary",)),
        name="peer_route",
    )(h2t, wq_t, k1, k2)


FFN_MXU_TILE = 256
FFN_VPU_TILE = LANES
BF16_ROWS = 16
FFN_ROW_CHUNK = 64


def _ffn_kernel(h_ref, u_ref, vt_ref, cnt_ref, m1_ref, r2_ref, e2_ref, o_ref, coef_ref, *act_refs):
    tm = h_ref.shape[1]
    n_bt = N_KEYS // BF16_ROWS

    def row16(ref, h, al, ls):
        row = ref[al // SUBLANES, h, al % SUBLANES:al % SUBLANES + 1, ls]
        return jnp.broadcast_to(row, (BF16_ROWS, FFN_VPU_TILE)).astype(BF16)

    def mxu_cols(mt):
        return slice(mt * FFN_MXU_TILE, (mt + 1) * FFN_MXU_TILE)

    def build_gates(mt):
        for vt in range(FFN_MXU_TILE // FFN_VPU_TILE):
            lo = mt * FFN_MXU_TILE + vt * FFN_VPU_TILE
            ls = slice(lo, lo + FFN_VPU_TILE)
            for a0 in range(0, EB_FFN // N_KEYS, 2):
                acc = [[jnp.zeros((BF16_ROWS, FFN_VPU_TILE), BF16) for _ in range(n_bt)] for _ in range(2)]
                for h in range(PEER_HEADS):
                    cnts = [row16(cnt_ref, h, a0 + k, ls) for k in range(2)]
                    m1s = [row16(m1_ref, h, a0 + k, ls) for k in range(2)]
                    for bt in range(n_bt):
                        rs = slice((h * N_KEYS + bt * BF16_ROWS) // 2, (h * N_KEYS + (bt + 1) * BF16_ROWS) // 2)
                        rank, e2 = _unpack(r2_ref[rs, ls]), _unpack(e2_ref[rs, ls])
                        for k in range(2):
                            acc[k][bt] = acc[k][bt] + jnp.where(rank < cnts[k], e2, jnp.zeros_like(e2)) * m1s[k]
                for k in range(2):
                    for bt in range(n_bt):
                        es = slice((a0 + k) * N_KEYS + bt * BF16_ROWS, (a0 + k) * N_KEYS + (bt + 1) * BF16_ROWS)
                        coef_ref[es, ls] = acc[k][bt]

    def up_proj(mt):
        act_refs[mt % 2][...] = jnp.dot(_unpack(u_ref[...]), _unpack(h_ref[:, mxu_cols(mt)]),
                                        preferred_element_type=F32)

    def down_proj(mt):
        ms = mxu_cols(mt)
        k0 = (2.0 / jnp.pi) ** 0.5
        zero = jnp.minimum(pl.program_id(1), 0)
        for rc in range(EB_FFN // FFN_ROW_CHUNK):
            rs = slice(rc * FFN_ROW_CHUNK, (rc + 1) * FFN_ROW_CHUNK)
            x = act_refs[mt % 2][pl.ds(pl.multiple_of(zero + rc * FFN_ROW_CHUNK, FFN_ROW_CHUNK), FFN_ROW_CHUNK), :]
            hx = 0.5 * x
            gelu = hx + hx * jnp.tanh(x * (k0 + (k0 * 0.044715) * (x * x)))
            coef_ref[rs, ms] = (coef_ref[rs, ms].astype(F32) * gelu).astype(BF16)
        o_ref[:, ms] += jnp.dot(_unpack(vt_ref[...]), coef_ref[:, ms], preferred_element_type=F32)

    n_mt = tm // FFN_MXU_TILE

    @pl.when(pl.program_id(1) == 0)
    def _():
        o_ref[...] = jnp.zeros_like(o_ref)

    build_gates(0)
    up_proj(0)
    for mt in range(n_mt):
        if mt + 1 < n_mt:
            build_gates(mt + 1)
            up_proj(mt + 1)
        down_proj(mt)


def _ffn(h2t, u_b, vt_b, cnt, m1, r2, e2):
    n = h2t.shape[1]
    tm, eb = TM_FFN, EB_FFN
    n_exp = 2 * u_b.shape[0]
    spec4 = pl.BlockSpec((eb // (N_KEYS * SUBLANES), PEER_HEADS, SUBLANES, tm), lambda i, e: (e, 0, 0, i))
    spec2 = pl.BlockSpec((PEER_HEADS * N_KEYS // 2, tm), lambda i, e: (0, i))
    return pl.pallas_call(
        _ffn_kernel,
        out_shape=jax.ShapeDtypeStruct((D_MODEL, n), F32),
        grid=(n // tm, n_exp // eb),
        in_specs=[pl.BlockSpec((D_MODEL // 2, tm), lambda i, e: (0, i)),
                  pl.BlockSpec((eb // 2, D_MODEL), lambda i, e: (e, 0)),
                  pl.BlockSpec((D_MODEL // 2, eb), lambda i, e: (0, e)),
                  spec4, spec4, spec2, spec2],
        out_specs=pl.BlockSpec((D_MODEL, tm), lambda i, e: (0, i)),
        scratch_shapes=[pltpu.VMEM((eb, tm), BF16)] + [pltpu.VMEM((eb, FFN_MXU_TILE), F32)] * 2,
        compiler_params=_cparams(("arbitrary", "arbitrary")),
        name="peer_ffn",
    )(h2t, u_b, vt_b, cnt, m1, r2, e2)


def _final_kernel(p_ref, x_ref, g2_ref, nf_ref, y_ref):
    x2 = x_ref[...] + g2_ref[...] * p_ref[...].T
    ms = jnp.mean(x2 * x2, axis=-1, keepdims=True)
    y_ref[...] = x2 * lax.rsqrt(ms + NORM_EPS) * nf_ref[...]


def _final(peer_t, col_off, x1, mod_spec, g2, nf):
    n = x1.shape[0]
    tm = TM_FINAL
    off = col_off // tm
    return pl.pallas_call(
        _final_kernel,
        out_shape=jax.ShapeDtypeStruct((n, D_MODEL), F32),
        grid=(n // tm,),
        in_specs=[pl.BlockSpec((D_MODEL, tm), lambda i: (0, i + off)),
                  pl.BlockSpec((tm, D_MODEL), lambda i: (i, 0)),
                  mod_spec,
                  pl.BlockSpec((1, D_MODEL), lambda i: (0, 0))],
        out_specs=pl.BlockSpec((tm, D_MODEL), lambda i: (i, 0)),
        compiler_params=_cparams(("arbitrary",)),
        name="final",
    )(peer_t, x1, g2, nf)


def _rope_tables(pos):
    half = HEAD_DIM // 2
    inv = ROPE_THETA ** (-jnp.arange(half, dtype=F32) / half)
    ang = pos.astype(F32)[:, None] * inv[None, :]
    cos, sin = jnp.cos(ang), jnp.sin(ang)
    zero = jnp.zeros_like(sin)
    reps = LANES // HEAD_DIM
    cos_t = jnp.tile(jnp.concatenate([cos, cos], axis=1), (1, reps))
    sa_t = jnp.tile(jnp.concatenate([-sin, zero], axis=1), (1, reps))
    sb_t = jnp.tile(jnp.concatenate([zero, sin], axis=1), (1, reps))
    return cos_t, sa_t, sb_t


def kernel(x_prompt, x_sample, cache_kv0, cache_kv1, cache_kv2, state_conv, c_prompt, c_sample,
           norm1_g, norm2_g, norm_f_g, w_ada, b_ada, w_in, conv_w, w_attn_o, w_conv_o, w_o,
           w_query, sub_keys, expert_u, expert_v):
    batch, seq, _ = x_prompt.shape
    n_seq, s_len, _ = x_sample.shape
    depth = w_in.shape[0]
    assert depth == 1
    assert tuple(c.shape[2] for c in (cache_kv0, cache_kv1, cache_kv2)) == WINDOWS
    assert s_len == SUBLANES and seq % (ATTN_STEP * DILATIONS[2]) == 0
    n_p, n_s = batch * seq, n_seq * s_len
    l = 0

    w_in_b = _pack_weight(w_in[l])
    wa_b, wc_b, wo_b = _pack_weight(w_attn_o[l]), _pack_weight(w_conv_o[l]), _pack_weight(w_o[l])
    wq_t = _pack_weight(w_query[l], transpose=True)
    k1_b, k2_b = sub_keys[l, 0].astype(BF16), sub_keys[l, 1].astype(BF16)
    u_b = _pack_weight(expert_u[l])
    vt_b = _pack_weight(expert_v[l], transpose=True)
    n1, n2, nf = norm1_g[l][None, :], norm2_g[l][None, :], norm_f_g[None, :]

    n_c = batch + n_seq
    c_all = jnp.concatenate([c_prompt, c_sample, jnp.zeros((-n_c % SUBLANES, D_MODEL), F32)], axis=0)
    mod = _ada(c_all, w_ada[l], b_ada[l])
    mod_p = [m[:, None, :] for m in jnp.split(mod[:batch], 6, axis=-1)]
    mod_s = jnp.split(jnp.repeat(mod[batch:n_c], s_len, axis=0), 6, axis=-1)

    def seq_spec(tm):
        return pl.BlockSpec((None, 1, D_MODEL), lambda i: (i // (seq // tm), 0, 0))

    def tok_spec(tm):
        return pl.BlockSpec((tm, D_MODEL), lambda i: (i, 0))

    tabs_p = _rope_tables(jnp.arange(seq))
    tabs_s = [jnp.tile(t, (n_seq, 1)) for t in _rope_tables(PAST_LEN + jnp.arange(s_len))]
    tab_p_spec = pl.BlockSpec((TM_PROJ, LANES), lambda i: (i % (seq // TM_PROJ), 0))
    tab_s_spec = pl.BlockSpec((TM_PROJ, LANES), lambda i: (i, 0))

    xp = x_prompt.reshape(n_p, D_MODEL)
    xs = x_sample.reshape(n_s, D_MODEL)
    *qkv_p, bgp, up, gap, gcp = _inproj(xp, seq_spec(TM_PROJ), (mod_p[1], mod_p[0]), tab_p_spec, tabs_p, n1, w_in_b,
                                        residue_seq=seq)
    qp, kp, vp = qkv_p[0:N_GROUPS], qkv_p[N_GROUPS:2 * N_GROUPS], qkv_p[2 * N_GROUPS:]
    qs, ks, vs, bgs, us, gas, gcs = _inproj(xs, tok_spec(TM_PROJ), (mod_s[1], mod_s[0]), tab_s_spec, tabs_s, n1, w_in_b)

    attn_p = [_pattn(qp[g], kp[g], vp[g], g) for g in range(N_GROUPS)]
    caches = (cache_kv0[l], cache_kv1[l], cache_kv2[l])
    attn_s = _sattn(qs, ks, vs, caches, n_seq, s_len)

    x1p, h2tp = _merge_prompt(attn_p, bgp, up, conv_w[l], gap, gcp, xp, seq_spec(TM_MERGE),
                              mod_p[2], n2, mod_p[4], mod_p[3], wa_b, wc_b, wo_b, seq)
    st = state_conv[l]
    p1 = jnp.pad(st[:, 1:2], ((0, 0), (0, s_len - 1), (0, 0))).reshape(n_s, CONV_WIDTH)
    p2 = jnp.pad(st, ((0, 0), (0, s_len - 2), (0, 0))).reshape(n_s, CONV_WIDTH)
    x1s, h2ts = _merge_sample(attn_s, bgs, us, p1, p2, conv_w[l], gas, gcs, xs, tok_spec(TM_MERGE),
                              mod_s[2], n2, mod_s[4], mod_s[3], wa_b, wc_b, wo_b, s_len)

    h2t = jnp.concatenate([h2tp, h2ts], axis=1)
    cnt, m1, r2, e2 = _route(h2t, wq_t, k1_b, k2_b)
    peer_t = _ffn(h2t, u_b, vt_b, cnt, m1, r2, e2)

    yp = _final(peer_t, 0, x1p, seq_spec(TM_FINAL), mod_p[5], nf).reshape(batch, seq, D_MODEL)
    ys = _final(peer_t, n_p, x1s, tok_spec(TM_FINAL), mod_s[5], nf).reshape(n_seq, s_len, D_MODEL)

    def prompt_tail(a, g):
        d, n_rows = a.shape[1], a.shape[2]
        t = jnp.swapaxes(a[:, :, n_rows - N_BACK:, :], 1, 2)
        return t.reshape(batch, N_BACK * d, HEADS_PER_GROUP, HEAD_DIM)

    kv_p, kv_s = [], []
    for g in range(N_GROUPS):
        assert WINDOWS[g] == N_BACK * DILATIONS[g] <= seq
        gs = slice(g * GROUP_WIDTH, (g + 1) * GROUP_WIDTH)
        kv_p.append(jnp.stack([prompt_tail(kp[g], g), prompt_tail(vp[g], g)], axis=2)[None])
        k4 = ks.reshape(n_seq, s_len, ATTN_WIDTH)[:, :, gs].reshape(n_seq, s_len, HEADS_PER_GROUP, HEAD_DIM)
        v4 = vs.reshape(n_seq, s_len, ATTN_WIDTH)[:, :, gs].reshape(n_seq, s_len, HEADS_PER_GROUP, HEAD_DIM)
        kv_s.append(jnp.stack([k4, v4], axis=2)[None])
    conv_p = up.reshape(batch, seq, CONV_WIDTH)[:, seq - (CONV_K - 1):][None]
    conv_s = jnp.concatenate([st, us.reshape(n_seq, s_len, CONV_WIDTH)], axis=1)[:, s_len:][None]
    return (yp, ys, kv_p[0], kv_p[1], kv_p[2], conv_p, kv_s[0], kv_s[1], kv_s[2], conv_s)
```

```python
import functools

import jax
import jax.numpy as jnp
from jax import lax
from jax.experimental import pallas as pl
from jax.experimental.pallas import tpu as pltpu

F32 = jnp.float32
BF16 = jnp.bfloat16
U32 = jnp.uint32

D_MODEL = 1024
HEAD_DIM = 64
HEADS_PER_GROUP = 4
WINDOWS = (128, 512, 2048)
DILATIONS = (1, 4, 16)
N_GROUPS = 3
GROUP_WIDTH = HEADS_PER_GROUP * HEAD_DIM
ATTN_WIDTH = N_GROUPS * GROUP_WIDTH
CONV_WIDTH = 768
CONV_K = 3
IN_WIDTH = 3 * ATTN_WIDTH + 3 * CONV_WIDTH + 2 * D_MODEL
ATTN_SCALE = HEAD_DIM ** -0.5
ROPE_THETA = 10000.0
N_KEYS = 128
PEER_HEADS = 8
PEER_TOPK = 16
NORM_EPS = 1e-6
LOG2_E = 1.4426950408889634
N_BACK = 128
PAST_LEN = 8192

LANES = 128
SUBLANES = 8
VMEM_LIMIT = 52 * 1024 * 1024

TM_PROJ = 512
TM_MERGE = 512
TM_FINAL = 512
TK_ROUTE = 256
TM_FFN = 1024
EB_FFN = 1024
ATTN_BLK = 128
ATTN_STEP = 512


def _unpack(packed):
    return pltpu.bitcast(packed, BF16)


def _pack_kernel(w_ref, o_ref, *, transpose):
    w = w_ref[...]
    if transpose:
        w = w.T
    o_ref[...] = pltpu.bitcast(w.astype(BF16), U32)


def _pack_weight(w, transpose=False):
    r, c = w.shape
    tr = next(t for t in (1024, 512, 256, 128) if r % t == 0)
    tc = next(t for t in (1024, 512, 256, 128) if c % t == 0)
    if transpose:
        out_shape, out_spec = (c // 2, r), pl.BlockSpec((tc // 2, tr), lambda i, j: (j, i))
    else:
        out_shape, out_spec = (r // 2, c), pl.BlockSpec((tr // 2, tc), lambda i, j: (i, j))
    return pl.pallas_call(
        functools.partial(_pack_kernel, transpose=transpose),
        out_shape=jax.ShapeDtypeStruct(out_shape, U32),
        grid=(r // tr, c // tc),
        in_specs=[pl.BlockSpec((tr, tc), lambda i, j: (i, j))],
        out_specs=out_spec,
        compiler_params=_cparams(("arbitrary", "arbitrary")),
        name="pack_weight",
    )(w)


def _cparams(sem, flags=None):
    return pltpu.CompilerParams(dimension_semantics=sem, vmem_limit_bytes=VMEM_LIMIT, flags=flags)


def _ada_kernel(c_ref, w_ref, b_ref, o_ref):
    c = c_ref[...]
    a = (c * jax.nn.sigmoid(c)).astype(BF16)
    o_ref[...] = jnp.dot(a, w_ref[...].astype(BF16), preferred_element_type=F32) + b_ref[...]


def _ada(c_all, w_ada, b_ada):
    rows = c_all.shape[0]
    n_out = w_ada.shape[1]
    tn = 1024
    return pl.pallas_call(
        _ada_kernel,
        out_shape=jax.ShapeDtypeStruct((rows, n_out), F32),
        grid=(n_out // tn,),
        in_specs=[pl.BlockSpec((rows, D_MODEL), lambda j: (0, 0)),
                  pl.BlockSpec((D_MODEL, tn), lambda j: (0, j)),
                  pl.BlockSpec((1, tn), lambda j: (0, j))],
        out_specs=pl.BlockSpec((rows, tn), lambda j: (0, j)),
        compiler_params=_cparams(("arbitrary",)),
        name="adaln",
    )(c_all, w_ada, b_ada.reshape(1, n_out))


def _rmsnorm_mod(x, g, scale, shift):
    ms = jnp.mean(x * x, axis=-1, keepdims=True)
    y = x * lax.rsqrt(ms + NORM_EPS) * g
    return y * (1.0 + scale) + shift


def _inproj_kernel(x_ref, sc_ref, sh_ref, n1_ref, cos_ref, sa_ref, sb_ref, w_ref, *refs, by_residue):
    n_qkv = 3 * N_GROUPS if by_residue else 3
    bg_ref, u_ref, ga_ref, gc_ref, zbuf = refs[n_qkv:]
    n_chunks, tm = zbuf.shape[0], zbuf.shape[1]
    per_group = GROUP_WIDTH // LANES
    h = _rmsnorm_mod(x_ref[...], n1_ref[...], sc_ref[...], sh_ref[...]).astype(BF16)

    def proj(lo, hi):
        return jnp.dot(h, _unpack(w_ref[:, lo:hi]), preferred_element_type=F32)

    cos, sa, sb = cos_ref[...], sa_ref[...], sb_ref[...]

    def stage(z, rope, mult):
        for c in range(n_chunks):
            zc = z[:, c * LANES:(c + 1) * LANES]
            if rope:
                zc = zc * cos + pltpu.roll(zc, LANES - 32, 1) * sa + pltpu.roll(zc, 32, 1) * sb
            zbuf[c] = zc * mult if mult != 1.0 else zc

    def emit(which):
        for c in range(n_chunks):
            if not by_residue:
                refs[which][:, c * LANES:(c + 1) * LANES] = zbuf[c].astype(refs[which].dtype)
                continue
            g, cl = c // per_group, c % per_group
            ref, d = refs[which * N_GROUPS + g], DILATIONS[g]
            for r in range(d):
                ref[r, :, cl * LANES:(cl + 1) * LANES] = zbuf[c, pl.ds(r, tm // d, stride=d), :].astype(ref.dtype)

    o = 0
    stage(proj(o, o + ATTN_WIDTH), True, ATTN_SCALE)
    emit(0)
    o += ATTN_WIDTH
    stage(proj(o, o + ATTN_WIDTH), True, 1.0)
    emit(1)
    o += ATTN_WIDTH
    stage(proj(o, o + ATTN_WIDTH), False, 1.0)
    emit(2)
    o += ATTN_WIDTH
    bg_ref[...] = proj(o, o + CONV_WIDTH)
    o += CONV_WIDTH
    cg = proj(o, o + CONV_WIDTH)
    o += CONV_WIDTH
    u_ref[...] = cg * proj(o, o + CONV_WIDTH)
    o += CONV_WIDTH
    ga_ref[...] = proj(o, o + D_MODEL)
    o += D_MODEL
    gc_ref[...] = proj(o, o + D_MODEL)


def _inproj(x2d, mod_specs, mods, tab_spec, tabs, n1, w_in_p, residue_seq=None):
    n = x2d.shape[0]
    tm = TM_PROJ
    row = lambda w: pl.BlockSpec((tm, w), lambda i: (i, 0))
    const = lambda s: pl.BlockSpec(s, lambda i: (0, 0), pipeline_mode=pl.Buffered(1))
    if residue_seq is None:
        qkv_shapes = [jax.ShapeDtypeStruct((n, ATTN_WIDTH), dt) for dt in (BF16, F32, F32)]
        qkv_specs = [row(ATTN_WIDTH)] * 3
    else:
        bps = residue_seq // tm
        qkv_shapes, qkv_specs = [], []
        for dt in (BF16, F32, F32):
            for d in DILATIONS:
                qkv_shapes.append(jax.ShapeDtypeStruct((n // residue_seq, d, residue_seq // d, GROUP_WIDTH), dt))
                qkv_specs.append(pl.BlockSpec((None, d, tm // d, GROUP_WIDTH), lambda i: (i // bps, 0, i % bps, 0)))
    outs = qkv_shapes + [jax.ShapeDtypeStruct((n, CONV_WIDTH), F32),
                         jax.ShapeDtypeStruct((n, CONV_WIDTH), F32),
                         jax.ShapeDtypeStruct((n, D_MODEL), F32),
                         jax.ShapeDtypeStruct((n, D_MODEL), F32)]
    return pl.pallas_call(
        functools.partial(_inproj_kernel, by_residue=residue_seq is not None),
        out_shape=outs,
        grid=(n // tm,),
        in_specs=[row(D_MODEL), mod_specs, mod_specs, const((1, D_MODEL)),
                  tab_spec, tab_spec, tab_spec, const(w_in_p.shape)],
        out_specs=qkv_specs + [row(CONV_WIDTH)] * 2 + [row(D_MODEL)] * 2,
        scratch_shapes=[pltpu.VMEM((ATTN_WIDTH // LANES, tm, LANES), F32)],
        compiler_params=_cparams(("arbitrary",)),
        name="inproj",
    )(x2d, mods[0], mods[1], n1, tabs[0], tabs[1], tabs[2], w_in_p)


def _pattn_kernel(q_ref, kp_ref, kc_ref, vp_ref, vc_ref, o_ref, l_ref):
    nb = pl.program_id(2)
    k = jnp.concatenate([kp_ref[...], kc_ref[...]], axis=0).astype(BF16)
    v = jnp.concatenate([vp_ref[...], vc_ref[...]], axis=0).astype(BF16)
    qi = lax.broadcasted_iota(jnp.int32, (ATTN_BLK, 2 * ATTN_BLK), 0)
    ki = lax.broadcasted_iota(jnp.int32, (ATTN_BLK, 2 * ATTN_BLK), 1)
    dist = qi + ATTN_BLK - ki
    band = (dist >= 0) & (dist <= N_BACK)
    band0 = band & (ki >= jnp.where(nb > 0, 0, ATTN_BLK))
    for sb in range(ATTN_STEP // ATTN_BLK):
        rows = slice(sb * ATTN_BLK, (sb + 1) * ATTN_BLK)
        keys = slice(sb * ATTN_BLK, (sb + 2) * ATTN_BLK)
        q = q_ref[rows, :]
        valid = band0 if sb == 0 else band
        for h in range(HEADS_PER_GROUP):
            sl = slice(h * HEAD_DIM, (h + 1) * HEAD_DIM)
            s = lax.dot_general(q[:, sl], k[keys, sl], (((1,), (1,)), ((), ())), preferred_element_type=F32)
            s = jnp.where(valid, s, -jnp.inf)
            m = jnp.max(s, axis=-1, keepdims=True)
            p = jnp.exp(s - m)
            den = jnp.sum(p, axis=-1, keepdims=True)
            o = jnp.dot(p.astype(BF16), v[keys, sl], preferred_element_type=F32) / den
            o_ref[rows, sl] = o
            l_ref[rows, sl] = jnp.broadcast_to(m + jnp.log(den), (ATTN_BLK, HEAD_DIM))


def _pattn(q, k, v, g):
    batch, d, n, _ = q.shape
    per_blk = ATTN_STEP // ATTN_BLK
    cur = pl.BlockSpec((None, None, ATTN_STEP, GROUP_WIDTH), lambda b, r, nb: (b, r, nb, 0))
    prev = pl.BlockSpec((None, None, ATTN_BLK, GROUP_WIDTH),
                        lambda b, r, nb: (b, r, jnp.maximum(nb * per_blk - 1, 0), 0))
    return pl.pallas_call(
        _pattn_kernel,
        out_shape=[jax.ShapeDtypeStruct((batch, d, n, GROUP_WIDTH), F32)] * 2,
        grid=(batch, d, n // ATTN_STEP),
        in_specs=[cur, prev, cur, prev, cur],
        out_specs=[cur, cur],
        compiler_params=_cparams(("arbitrary",) * 3),
        name=f"pattn{g}",
    )(q, k, k, v, v)


def _sattn_kernel(q_ref, k_ref, v_ref, c0_ref, c1_ref, c2_ref, o_ref, *, s_len):
    q = q_ref[...].astype(F32)
    kn = k_ref[...]
    vn = v_ref[...]
    rows = HEADS_PER_GROUP * s_len
    lane_head = lax.broadcasted_iota(jnp.int32, (s_len, GROUP_WIDTH), 1) // HEAD_DIM
    pad = jnp.zeros((LANES - s_len, GROUP_WIDTH), F32)
    nt_dims = (((1,), (1,)), ((), ()))
    ms, ls, os_ = [], [], []
    for g, cref in enumerate((c0_ref, c1_ref, c2_ref)):
        d, win = DILATIONS[g], WINDOWS[g]
        gs = slice(g * GROUP_WIDTH, (g + 1) * GROUP_WIDTH)
        qg = q[:, gs]
        qexp = jnp.concatenate(
            [jnp.where(lane_head == h, qg, 0.0) for h in range(HEADS_PER_GROUP)], axis=0).astype(BF16)
        s_c = jnp.dot(qexp, cref[0].astype(BF16), preferred_element_type=F32)
        s_n = lax.dot_general(qexp, jnp.concatenate([kn[:, gs], pad], axis=0).astype(BF16), nt_dims,
                              preferred_element_type=F32)
        back_c = (win + lax.broadcasted_iota(jnp.int32, (rows, win), 0) % s_len
                  - lax.broadcasted_iota(jnp.int32, (rows, win), 1))
        col_n = lax.broadcasted_iota(jnp.int32, (rows, LANES), 1)
        back_n = lax.broadcasted_iota(jnp.int32, (rows, LANES), 0) % s_len - col_n
        s_c = jnp.where((back_c <= win) & ((back_c & (d - 1)) == 0), s_c, -jnp.inf)
        s_n = jnp.where((back_n >= 0) & ((back_n & (d - 1)) == 0) & (col_n < s_len), s_n, -jnp.inf)
        m = jnp.maximum(jnp.max(s_c, axis=-1, keepdims=True), jnp.max(s_n, axis=-1, keepdims=True))
        p_c = jnp.exp(s_c - m)
        p_n = jnp.exp(s_n - m)
        ls.append(jnp.sum(p_c, axis=-1, keepdims=True) + jnp.sum(p_n, axis=-1, keepdims=True))
        ms.append(m)
        o = lax.dot_general(p_c.astype(BF16), cref[1].astype(BF16), nt_dims, preferred_element_type=F32)
        o = o + jnp.dot(p_n.astype(BF16), jnp.concatenate([vn[:, gs], pad], axis=0).astype(BF16),
                        preferred_element_type=F32)
        os_.append(o)
    mm = jnp.maximum(jnp.maximum(ms[0], ms[1]), ms[2])
    num = jnp.zeros((rows, GROUP_WIDTH), F32)
    den = jnp.zeros((rows, 1), F32)
    for g in range(N_GROUPS):
        w = jnp.exp(ms[g] - mm)
        num = num + w * os_[g]
        den = den + w * ls[g]
    full = num / den
    out = jnp.zeros((s_len, GROUP_WIDTH), F32)
    for h in range(HEADS_PER_GROUP):
        out = out + jnp.where(lane_head == h, full[h * s_len:(h + 1) * s_len], 0.0)
    o_ref[...] = out


def _sattn(q, k, v, caches, n_seq, s_len):
    q3 = q.reshape(n_seq, s_len, ATTN_WIDTH)
    k3 = k.reshape(n_seq, s_len, ATTN_WIDTH)
    v3 = v.reshape(n_seq, s_len, ATTN_WIDTH)
    cts = [jnp.transpose(c, (0, 2, 3, 4, 1)).reshape(n_seq, 2, GROUP_WIDTH, c.shape[1]) for c in caches]
    tok = pl.BlockSpec((None, s_len, ATTN_WIDTH), lambda b: (b, 0, 0))
    out = pl.pallas_call(
        functools.partial(_sattn_kernel, s_len=s_len),
        out_shape=jax.ShapeDtypeStruct((n_seq, s_len, GROUP_WIDTH), F32),
        grid=(n_seq,),
        in_specs=[tok, tok, tok] + [pl.BlockSpec((None, 2, GROUP_WIDTH, w), lambda b: (b, 0, 0, 0)) for w in WINDOWS],
        out_specs=pl.BlockSpec((None, s_len, GROUP_WIDTH), lambda b: (b, 0, 0)),
        compiler_params=_cparams(("arbitrary",)),
        name="sattn",
    )(q3, k3, v3, *cts)
    return out.reshape(n_seq * s_len, GROUP_WIDTH)


def _merge_tail(o_attn, bg, yc, ga_ref, gc_ref, x_ref, g1_ref, n2_ref, sc2_ref, sh2_ref,
                wa_ref, wc_ref, wo_ref, x1_ref, h2t_ref):
    a_out = jnp.dot(o_attn.astype(BF16), _unpack(wa_ref[...]), preferred_element_type=F32)
    c_out = jnp.dot((bg * yc).astype(BF16), _unpack(wc_ref[...]), preferred_element_type=F32)
    mix = jax.nn.sigmoid(ga_ref[...]) * a_out + jax.nn.sigmoid(gc_ref[...]) * c_out
    mo = jnp.dot(mix.astype(BF16), _unpack(wo_ref[...]), preferred_element_type=F32)
    x1 = x_ref[...] + g1_ref[...] * mo
    x1_ref[...] = x1
    h2 = _rmsnorm_mod(x1, n2_ref[...], sc2_ref[...], sh2_ref[...])
    h2t_ref[...] = pltpu.bitcast(h2.T.astype(BF16), U32)


def _conv3(u, um1, um2, cw_ref):
    return cw_ref[0:1, :] * um2 + cw_ref[1:2, :] * um1 + cw_ref[2:3, :] * u


def _merge_prompt_kernel(o0_ref, o1_ref, o2_ref, l0_ref, l1_ref, l2_ref, bg_ref, u_ref, uh_ref, cw_ref,
                         ga_ref, gc_ref, x_ref, g1_ref, n2_ref, sc2_ref, sh2_ref, wa_ref, wc_ref, wo_ref,
                         x1_ref, h2t_ref, ubuf, *rowbufs, blocks_per_seq):
    tm = u_ref.shape[0]
    first = (pl.program_id(0) % blocks_per_seq) == 0

    def by_position(ref, buf):
        d, rows = ref.shape[0], ref.shape[1]
        if d == 1:
            return ref[0]
        for r in range(d):
            for c in range(GROUP_WIDTH // LANES):
                buf[c, pl.ds(r, rows, stride=d), :] = ref[r, :, c * LANES:(c + 1) * LANES]
        return jnp.concatenate([buf[c] for c in range(GROUP_WIDTH // LANES)], axis=1)

    o0, l0 = by_position(o0_ref, None), by_position(l0_ref, None)
    o1, l1 = by_position(o1_ref, rowbufs[0]), by_position(l1_ref, rowbufs[1])
    o2, l2 = by_position(o2_ref, rowbufs[2]), by_position(l2_ref, rowbufs[3])
    mm = jnp.maximum(jnp.maximum(l0, l1), l2)
    e0, e1, e2 = jnp.exp(l0 - mm), jnp.exp(l1 - mm), jnp.exp(l2 - mm)
    o_attn = (e0 * o0 + e1 * o1 + e2 * o2) / (e0 + e1 + e2)
    u = u_ref[...]
    ubuf[0:SUBLANES, :] = jnp.where(first, 0.0, uh_ref[...])
    ubuf[SUBLANES:SUBLANES + tm, :] = u
    yc = _conv3(u, ubuf[SUBLANES - 1:SUBLANES - 1 + tm, :], ubuf[SUBLANES - 2:SUBLANES - 2 + tm, :], cw_ref)
    _merge_tail(o_attn, bg_ref[...], yc, ga_ref, gc_ref, x_ref, g1_ref, n2_ref, sc2_ref, sh2_ref,
                wa_ref, wc_ref, wo_ref, x1_ref, h2t_ref)


def _merge_sample_kernel(oa_ref, bg_ref, u_ref, p1_ref, p2_ref, cw_ref,
                         ga_ref, gc_ref, x_ref, g1_ref, n2_ref, sc2_ref, sh2_ref, wa_ref, wc_ref, wo_ref,
                         x1_ref, h2t_ref, ubuf, *, s_len):
    tm = u_ref.shape[0]
    u = u_ref[...]
    ubuf[0:SUBLANES, :] = jnp.zeros((SUBLANES, CONV_WIDTH), F32)
    ubuf[SUBLANES:SUBLANES + tm, :] = u
    t = lax.broadcasted_iota(jnp.int32, (tm, CONV_WIDTH), 0) % s_len
    um1 = jnp.where(t < 1, p1_ref[...], ubuf[SUBLANES - 1:SUBLANES - 1 + tm, :])
    um2 = jnp.where(t < 2, p2_ref[...], ubuf[SUBLANES - 2:SUBLANES - 2 + tm, :])
    yc = _conv3(u, um1, um2, cw_ref)
    _merge_tail(oa_ref[...], bg_ref[...], yc, ga_ref, gc_ref, x_ref, g1_ref, n2_ref, sc2_ref, sh2_ref,
                wa_ref, wc_ref, wo_ref, x1_ref, h2t_ref)


def _merge_common_specs(tm, mod_spec):
    row = lambda w: pl.BlockSpec((tm, w), lambda i: (i, 0))
    const = lambda s: pl.BlockSpec(s, lambda i: (0, 0), pipeline_mode=pl.Buffered(1))
    ins = [row(D_MODEL), row(D_MODEL), row(D_MODEL), mod_spec, const((1, D_MODEL)), mod_spec, mod_spec,
           const((GROUP_WIDTH // 2, D_MODEL)), const((CONV_WIDTH // 2, D_MODEL)), const((D_MODEL // 2, D_MODEL))]
    outs = [row(D_MODEL), pl.BlockSpec((D_MODEL // 2, tm), lambda i: (0, i))]
    return ins, outs


def _merge_prompt(attn, bg, u, conv_w, ga, gc, x2d, mod_spec, g1, n2, sc2, sh2, wa, wc, wo, seq):
    n = x2d.shape[0]
    tm = TM_MERGE
    row = lambda w: pl.BlockSpec((tm, w), lambda i: (i, 0))
    halo = pl.BlockSpec((SUBLANES, CONV_WIDTH), lambda i: (jnp.maximum(i * (tm // SUBLANES) - 1, 0), 0))
    ins, outs = _merge_common_specs(tm, mod_spec)
    (o0, l0), (o1, l1), (o2, l2) = attn
    bps = seq // tm
    grp = [pl.BlockSpec((None, d, tm // d, GROUP_WIDTH), lambda i: (i // bps, 0, i % bps, 0)) for d in DILATIONS]
    return pl.pallas_call(
        functools.partial(_merge_prompt_kernel, blocks_per_seq=bps),
        out_shape=[jax.ShapeDtypeStruct((n, D_MODEL), F32), jax.ShapeDtypeStruct((D_MODEL // 2, n), U32)],
        grid=(n // tm,),
        in_specs=grp + grp + [row(CONV_WIDTH), row(CONV_WIDTH), halo,
                              pl.BlockSpec((CONV_K, CONV_WIDTH), lambda i: (0, 0))] + ins,
        out_specs=outs,
        scratch_shapes=[pltpu.VMEM((SUBLANES + tm, CONV_WIDTH), F32)] + [pltpu.VMEM((GROUP_WIDTH // LANES, tm, LANES), F32)] * 4,
        compiler_params=_cparams(("arbitrary",)),
        name="merge_prompt",
    )(o0, o1, o2, l0, l1, l2, bg, u, u, conv_w, ga, gc, x2d, g1, n2, sc2, sh2, wa, wc, wo)


def _merge_sample(o_attn, bg, u, p1, p2, conv_w, ga, gc, x2d, mod_spec, g1, n2, sc2, sh2, wa, wc, wo, s_len):
    n = x2d.shape[0]
    tm = TM_MERGE
    row = lambda w: pl.BlockSpec((tm, w), lambda i: (i, 0))
    ins, outs = _merge_common_specs(tm, mod_spec)
    return pl.pallas_call(
        functools.partial(_merge_sample_kernel, s_len=s_len),
        out_shape=[jax.ShapeDtypeStruct((n, D_MODEL), F32), jax.ShapeDtypeStruct((D_MODEL // 2, n), U32)],
        grid=(n // tm,),
        in_specs=[row(GROUP_WIDTH)] + [row(CONV_WIDTH)] * 4 + [pl.BlockSpec((CONV_K, CONV_WIDTH), lambda i: (0, 0))] + ins,
        out_specs=outs,
        scratch_shapes=[pltpu.VMEM((SUBLANES + tm, CONV_WIDTH), F32)],
        compiler_params=_cparams(("arbitrary",)),
        name="merge_sample",
    )(o_attn, bg, u, p1, p2, conv_w, ga, gc, x2d, g1, n2, sc2, sh2, wa, wc, wo)


def _oddeven_merge_sort_pairs(n):
    pairs = []
    p = 1
    while p < n:
        k = p
        while k >= 1:
            for j in range(k % p, n - k, 2 * k):
                for i in range(min(k, n - j - k)):
                    if (i + j) // (2 * p) == (i + j + k) // (2 * p):
                        pairs.append((i + j, i + j + k))
            k //= 2
        p *= 2
    return pairs


_SORT16 = _oddeven_merge_sort_pairs(PEER_TOPK)


def _cmpx(x, a, b):
    hi, lo = jnp.maximum(x[a], x[b]), jnp.minimum(x[a], x[b])
    x[a], x[b] = hi, lo


def _bitonic_clean(x):
    stride = PEER_TOPK // 2
    while stride >= 1:
        for i in range(PEER_TOPK):
            if i & stride == 0:
                _cmpx(x, i, i + stride)
        stride //= 2
    return x


def _merge_top16(a, b):
    c = []
    for j in range(PEER_TOPK):
        jb = PEER_TOPK - 1 - j
        c.append(jnp.maximum(a[j], b[jb]) if jb < len(b) else a[j])
    return _bitonic_clean(c)


def _top16_over_keys(s):
    x = [s[j * SUBLANES:(j + 1) * SUBLANES, :] for j in range(N_KEYS // SUBLANES)]
    for a, b in _SORT16:
        _cmpx(x, a, b)
    shift = SUBLANES // 2
    while shift >= 1:
        x = _merge_top16(x, [pltpu.roll(v, shift, 0) for v in x])
        shift //= 2
    return x


def _route_kernel(h_ref, wq_ref, k1_ref, k2_ref, cnt_ref, m1_ref, r2_ref, e2_ref):
    tk = h_ref.shape[1]
    qt = jnp.dot(_unpack(wq_ref[...]), _unpack(h_ref[...]), preferred_element_type=F32).astype(BF16)
    k1, k2 = k1_ref[...], k2_ref[...]
    sub = lax.broadcasted_iota(jnp.int32, (SUBLANES, tk), 0)
    half = N_KEYS
    s1s, s2s = [], []
    v1 = [jnp.zeros((SUBLANES, tk), F32) for _ in range(PEER_TOPK)]
    v2 = [jnp.zeros((SUBLANES, tk), F32) for _ in range(PEER_TOPK)]
    for h in range(PEER_HEADS):
        base = h * 2 * half
        s1 = jnp.dot(k1, qt[base:base + half], preferred_element_type=F32)
        s2 = jnp.dot(k2, qt[base + half:base + 2 * half], preferred_element_type=F32)
        s1s.append(s1)
        s2s.append(s2)
        t1 = _top16_over_keys(s1)
        t2 = _top16_over_keys(s2)
        for j in range(PEER_TOPK):
            v1[j] = jnp.where(sub == h, t1[j], v1[j])
            v2[j] = jnp.where(sub == h, t2[j], v2[j])
    psum = {}
    for a in range(PEER_TOPK):
        for b in range(PEER_TOPK // (a + 1)):
            psum[a, b] = v1[a] + v2[b]
    lists = [[psum[a, b] for b in range(PEER_TOPK // (a + 1))] for a in range(PEER_TOPK // 2)]
    lists.append([psum[a, 0] for a in range(PEER_TOPK // 2, PEER_TOPK)])
    top = lists[0]
    for other in lists[1:]:
        top = _merge_top16(top, other)
    tau = top[PEER_TOPK - 1]
    z = jnp.ones((SUBLANES, tk), F32)
    for j in range(1, PEER_TOPK):
        z = z + jnp.exp(top[j] - top[0])
    rz = 1.0 / z
    x1 = []
    for b in range(PEER_TOPK):
        x = jnp.full((SUBLANES, tk), jnp.inf, F32)
        for a in range(PEER_TOPK // (b + 1)):
            x = jnp.where(psum[a, b] >= tau, v1[a], x)
        x1.append(x)
    for h in range(PEER_HEADS):
        s1, s2 = s1s[h], s2s[h]
        row = lambda v: v[h:h + 1, :]
        cnt = jnp.zeros_like(s1)
        rank = jnp.full_like(s2, float(PEER_TOPK))
        for j in range(PEER_TOPK):
            cnt = jnp.where(s1 >= row(x1[j]), float(j + 1), cnt)
        for j in range(PEER_TOPK - 1, -1, -1):
            rank = jnp.where(s2 >= row(v2[j]), float(j), rank)
        m1 = jnp.exp(s1 - row(v1[0])) * row(rz)
        cnt_ref[:, h, :, :] = cnt.reshape(N_KEYS // SUBLANES, SUBLANES, tk)
        m1_ref[:, h, :, :] = m1.reshape(N_KEYS // SUBLANES, SUBLANES, tk)
        half_rows = slice(h * N_KEYS // 2, (h + 1) * N_KEYS // 2)
        r2_ref[half_rows, :] = pltpu.bitcast(rank.astype(BF16), U32)
        e2_ref[half_rows, :] = pltpu.bitcast(jnp.exp(s2 - row(v2[0])).astype(BF16), U32)


def _route(h2t, wq_t, k1, k2):
    n = h2t.shape[1]
    tk = TK_ROUTE
    a8 = N_KEYS // SUBLANES
    s14 = jax.ShapeDtypeStruct((a8, PEER_HEADS, SUBLANES, n), F32)
    s2d = jax.ShapeDtypeStruct((PEER_HEADS * N_KEYS // 2, n), U32)
    spec4 = pl.BlockSpec((a8, PEER_HEADS, SUBLANES, tk), lambda i: (0, 0, 0, i))
    spec2 = pl.BlockSpec((PEER_HEADS * N_KEYS // 2, tk), lambda i: (0, i))
    return pl.pallas_call(
        _route_kernel,
        out_shape=[s14, s14, s2d, s2d],
        grid=(n // tk,),
        in_specs=[pl.BlockSpec((D_MODEL // 2, tk), lambda i: (0, i)),
                  pl.BlockSpec(wq_t.shape, lambda i: (0, 0)),
                  pl.BlockSpec(k1.shape, lambda i: (0, 0)),
                  pl.BlockSpec(k2.shape, lambda i: (0, 0))],
        out_specs=[spec4, spec4, spec2, spec2],
        compiler_params=_cparams(("arbitrary",)),
        name="peer_route",
    )(h2t, wq_t, k1, k2)


FFN_MXU_TILE = 256
FFN_VPU_TILE = LANES
BF16_ROWS = 16
FFN_ROW_CHUNK = 64


def _ffn_kernel(h_ref, u_ref, vt_ref, cnt_ref, m1_ref, r2_ref, e2_ref, o_ref, coef_ref, *act_refs):
    tm = h_ref.shape[1]
    n_bt = N_KEYS // BF16_ROWS

    def row16(ref, h, al, ls):
        row = ref[al // SUBLANES, h, al % SUBLANES:al % SUBLANES + 1, ls]
        return jnp.broadcast_to(row, (BF16_ROWS, FFN_VPU_TILE)).astype(BF16)

    def mxu_cols(mt):
        return slice(mt * FFN_MXU_TILE, (mt + 1) * FFN_MXU_TILE)

    def build_gates(mt):
        for vt in range(FFN_MXU_TILE // FFN_VPU_TILE):
            lo = mt * FFN_MXU_TILE + vt * FFN_VPU_TILE
            ls = slice(lo, lo + FFN_VPU_TILE)
            for a0 in range(0, EB_FFN // N_KEYS, 2):
                acc = [[jnp.zeros((BF16_ROWS, FFN_VPU_TILE), BF16) for _ in range(n_bt)] for _ in range(2)]
                for h in range(PEER_HEADS):
                    cnts = [row16(cnt_ref, h, a0 + k, ls) for k in range(2)]
                    m1s = [row16(m1_ref, h, a0 + k, ls) for k in range(2)]
                    for bt in range(n_bt):
                        rs = slice((h * N_KEYS + bt * BF16_ROWS) // 2, (h * N_KEYS + (bt + 1) * BF16_ROWS) // 2)
                        rank, e2 = _unpack(r2_ref[rs, ls]), _unpack(e2_ref[rs, ls])
                        for k in range(2):
                            acc[k][bt] = acc[k][bt] + jnp.where(rank < cnts[k], e2, jnp.zeros_like(e2)) * m1s[k]
                for k in range(2):
                    for bt in range(n_bt):
                        es = slice((a0 + k) * N_KEYS + bt * BF16_ROWS, (a0 + k) * N_KEYS + (bt + 1) * BF16_ROWS)
                        coef_ref[es, ls] = acc[k][bt]

    def up_proj(mt):
        act_refs[mt % 2][...] = jnp.dot(_unpack(u_ref[...]), _unpack(h_ref[:, mxu_cols(mt)]),
                                        preferred_element_type=F32)

    def down_proj(mt):
        ms = mxu_cols(mt)
        k0 = (2.0 / jnp.pi) ** 0.5
        zero = jnp.minimum(pl.program_id(1), 0)
        for rc in range(EB_FFN // FFN_ROW_CHUNK):
            rs = slice(rc * FFN_ROW_CHUNK, (rc + 1) * FFN_ROW_CHUNK)
            x = act_refs[mt % 2][pl.ds(pl.multiple_of(zero + rc * FFN_ROW_CHUNK, FFN_ROW_CHUNK), FFN_ROW_CHUNK), :]
            c0 = -2.0 * k0 * LOG2_E
            gelu = x / (1.0 + jnp.exp2(x * (c0 + (c0 * 0.044715) * (x * x))))
            coef_ref[rs, ms] = coef_ref[rs, ms] * gelu.astype(BF16)
        o_ref[:, ms] += jnp.dot(_unpack(vt_ref[...]), coef_ref[:, ms], preferred_element_type=F32)

    @pl.when(pl.program_id(1) == 0)
    def _():
        o_ref[...] = jnp.zeros_like(o_ref)

    n_mt = tm // FFN_MXU_TILE
    build_gates(0)
    up_proj(0)
    for mt in range(n_mt):
        if mt + 1 < n_mt:
            build_gates(mt + 1)
            up_proj(mt + 1)
        down_proj(mt)


def _ffn(h2t, u_b, vt_b, cnt, m1, r2, e2):
    n = h2t.shape[1]
    tm, eb = TM_FFN, EB_FFN
    n_exp = 2 * u_b.shape[0]
    spec4 = pl.BlockSpec((eb // (N_KEYS * SUBLANES), PEER_HEADS, SUBLANES, tm), lambda i, e: (e, 0, 0, i))
    spec2 = pl.BlockSpec((PEER_HEADS * N_KEYS // 2, tm), lambda i, e: (0, i))
    return pl.pallas_call(
        _ffn_kernel,
        out_shape=jax.ShapeDtypeStruct((D_MODEL, n), F32),
        grid=(n // tm, n_exp // eb),
        in_specs=[pl.BlockSpec((D_MODEL // 2, tm), lambda i, e: (0, i)),
                  pl.BlockSpec((eb // 2, D_MODEL), lambda i, e: (e, 0)),
                  pl.BlockSpec((D_MODEL // 2, eb), lambda i, e: (0, e)),
                  spec4, spec4, spec2, spec2],
        out_specs=pl.BlockSpec((D_MODEL, tm), lambda i, e: (0, i)),
        scratch_shapes=[pltpu.VMEM((eb, tm), BF16)] + [pltpu.VMEM((eb, FFN_MXU_TILE), F32)] * 2,
        compiler_params=_cparams(("arbitrary", "arbitrary")),
        name="peer_ffn",
    )(h2t, u_b, vt_b, cnt, m1, r2, e2)


def _final_kernel(p_ref, x_ref, g2_ref, nf_ref, y_ref):
    x2 = x_ref[...] + g2_ref[...] * p_ref[...].T
    ms = jnp.mean(x2 * x2, axis=-1, keepdims=True)
    y_ref[...] = x2 * lax.rsqrt(ms + NORM_EPS) * nf_ref[...]


def _final(peer_t, col_off, x1, mod_spec, g2, nf):
    n = x1.shape[0]
    tm = TM_FINAL
    off = col_off // tm
    return pl.pallas_call(
        _final_kernel,
        out_shape=jax.ShapeDtypeStruct((n, D_MODEL), F32),
        grid=(n // tm,),
        in_specs=[pl.BlockSpec((D_MODEL, tm), lambda i: (0, i + off)),
                  pl.BlockSpec((tm, D_MODEL), lambda i: (i, 0)),
                  mod_spec,
                  pl.BlockSpec((1, D_MODEL), lambda i: (0, 0))],
        out_specs=pl.BlockSpec((tm, D_MODEL), lambda i: (i, 0)),
        compiler_params=_cparams(("arbitrary",)),
        name="final",
    )(peer_t, x1, g2, nf)


def _rope_tables(pos):
    half = HEAD_DIM // 2
    inv = ROPE_THETA ** (-jnp.arange(half, dtype=F32) / half)
    ang = pos.astype(F32)[:, None] * inv[None, :]
    cos, sin = jnp.cos(ang), jnp.sin(ang)
    zero = jnp.zeros_like(sin)
    reps = LANES // HEAD_DIM
    cos_t = jnp.tile(jnp.concatenate([cos, cos], axis=1), (1, reps))
    sa_t = jnp.tile(jnp.concatenate([-sin, zero], axis=1), (1, reps))
    sb_t = jnp.tile(jnp.concatenate([zero, sin], axis=1), (1, reps))
    return cos_t, sa_t, sb_t


def kernel(x_prompt, x_sample, cache_kv0, cache_kv1, cache_kv2, state_conv, c_prompt, c_sample,
           norm1_g, norm2_g, norm_f_g, w_ada, b_ada, w_in, conv_w, w_attn_o, w_conv_o, w_o,
           w_query, sub_keys, expert_u, expert_v):
    batch, seq, _ = x_prompt.shape
    n_seq, s_len, _ = x_sample.shape
    depth = w_in.shape[0]
    assert depth == 1
    assert tuple(c.shape[2] for c in (cache_kv0, cache_kv1, cache_kv2)) == WINDOWS
    assert s_len == SUBLANES and seq % (ATTN_STEP * DILATIONS[2]) == 0
    n_p, n_s = batch * seq, n_seq * s_len
    l = 0

    w_in_b = _pack_weight(w_in[l])
    wa_b, wc_b, wo_b = _pack_weight(w_attn_o[l]), _pack_weight(w_conv_o[l]), _pack_weight(w_o[l])
    wq_t = _pack_weight(w_query[l], transpose=True)
    k1_b, k2_b = sub_keys[l, 0].astype(BF16), sub_keys[l, 1].astype(BF16)
    u_b = _pack_weight(expert_u[l])
    vt_b = _pack_weight(expert_v[l], transpose=True)
    n1, n2, nf = norm1_g[l][None, :], norm2_g[l][None, :], norm_f_g[None, :]

    n_c = batch + n_seq
    c_all = jnp.concatenate([c_prompt, c_sample, jnp.zeros((-n_c % SUBLANES, D_MODEL), F32)], axis=0)
    mod = _ada(c_all, w_ada[l], b_ada[l])
    mod_p = [m[:, None, :] for m in jnp.split(mod[:batch], 6, axis=-1)]
    mod_s = jnp.split(jnp.repeat(mod[batch:n_c], s_len, axis=0), 6, axis=-1)

    def seq_spec(tm):
        return pl.BlockSpec((None, 1, D_MODEL), lambda i: (i // (seq // tm), 0, 0))

    def tok_spec(tm):
        return pl.BlockSpec((tm, D_MODEL), lambda i: (i, 0))

    tabs_p = _rope_tables(jnp.arange(seq))
    tabs_s = [jnp.tile(t, (n_seq, 1)) for t in _rope_tables(PAST_LEN + jnp.arange(s_len))]
    tab_p_spec = pl.BlockSpec((TM_PROJ, LANES), lambda i: (i % (seq // TM_PROJ), 0))
    tab_s_spec = pl.BlockSpec((TM_PROJ, LANES), lambda i: (i, 0))

    xp = x_prompt.reshape(n_p, D_MODEL)
    xs = x_sample.reshape(n_s, D_MODEL)
    *qkv_p, bgp, up, gap, gcp = _inproj(xp, seq_spec(TM_PROJ), (mod_p[1], mod_p[0]), tab_p_spec, tabs_p, n1, w_in_b,
                                        residue_seq=seq)
    qp, kp, vp = qkv_p[0:N_GROUPS], qkv_p[N_GROUPS:2 * N_GROUPS], qkv_p[2 * N_GROUPS:]
    qs, ks, vs, bgs, us, gas, gcs = _inproj(xs, tok_spec(TM_PROJ), (mod_s[1], mod_s[0]), tab_s_spec, tabs_s, n1, w_in_b)

    attn_p = [_pattn(qp[g], kp[g], vp[g], g) for g in range(N_GROUPS)]
    caches = (cache_kv0[l], cache_kv1[l], cache_kv2[l])
    attn_s = _sattn(qs, ks, vs, caches, n_seq, s_len)

    x1p, h2tp = _merge_prompt(attn_p, bgp, up, conv_w[l], gap, gcp, xp, seq_spec(TM_MERGE),
                              mod_p[2], n2, mod_p[4], mod_p[3], wa_b, wc_b, wo_b, seq)
    st = state_conv[l]
    p1 = jnp.pad(st[:, 1:2], ((0, 0), (0, s_len - 1), (0, 0))).reshape(n_s, CONV_WIDTH)
    p2 = jnp.pad(st, ((0, 0), (0, s_len - 2), (0, 0))).reshape(n_s, CONV_WIDTH)
    x1s, h2ts = _merge_sample(attn_s, bgs, us, p1, p2, conv_w[l], gas, gcs, xs, tok_spec(TM_MERGE),
                              mod_s[2], n2, mod_s[4], mod_s[3], wa_b, wc_b, wo_b, s_len)

    h2t = jnp.concatenate([h2tp, h2ts], axis=1)
    cnt, m1, r2, e2 = _route(h2t, wq_t, k1_b, k2_b)
    peer_t = _ffn(h2t, u_b, vt_b, cnt, m1, r2, e2)

    yp = _final(peer_t, 0, x1p, seq_spec(TM_FINAL), mod_p[5], nf).reshape(batch, seq, D_MODEL)
    ys = _final(peer_t, n_p, x1s, tok_spec(TM_FINAL), mod_s[5], nf).reshape(n_seq, s_len, D_MODEL)

    def prompt_tail(a, g):
        d, n_rows = a.shape[1], a.shape[2]
        t = jnp.swapaxes(a[:, :, n_rows - N_BACK:, :], 1, 2)
        return t.reshape(batch, N_BACK * d, HEADS_PER_GROUP, HEAD_DIM)

    kv_p, kv_s = [], []
    for g in range(N_GROUPS):
        assert WINDOWS[g] == N_BACK * DILATIONS[g] <= seq
        gs = slice(g * GROUP_WIDTH, (g + 1) * GROUP_WIDTH)
        kv_p.append(jnp.stack([prompt_tail(kp[g], g), prompt_tail(vp[g], g)], axis=2)[None])
        k4 = ks.reshape(n_seq, s_len, ATTN_WIDTH)[:, :, gs].reshape(n_seq, s_len, HEADS_PER_GROUP, HEAD_DIM)
        v4 = vs.reshape(n_seq, s_len, ATTN_WIDTH)[:, :, gs].reshape(n_seq, s_len, HEADS_PER_GROUP, HEAD_DIM)
        kv_s.append(jnp.stack([k4, v4], axis=2)[None])
    conv_p = up.reshape(batch, seq, CONV_WIDTH)[:, seq - (CONV_K - 1):][None]
    conv_s = jnp.concatenate([st, us.reshape(n_seq, s_len, CONV_WIDTH)], axis=1)[:, s_len:][None]
    return (yp, ys, kv_p[0], kv_p[1], kv_p[2], conv_p, kv_s[0], kv_s[1], kv_s[2], conv_s)
```

```python
import functools

import jax
import jax.numpy as jnp
from jax import lax
from jax.experimental import pallas as pl
from jax.experimental.pallas import tpu as pltpu

F32 = jnp.float32
BF16 = jnp.bfloat16
U32 = jnp.uint32

D_MODEL = 1024
HEAD_DIM = 64
HEADS_PER_GROUP = 4
WINDOWS = (128, 512, 2048)
DILATIONS = (1, 4, 16)
N_GROUPS = 3
GROUP_WIDTH = HEADS_PER_GROUP * HEAD_DIM
ATTN_WIDTH = N_GROUPS * GROUP_WIDTH
CONV_WIDTH = 768
CONV_K = 3
IN_WIDTH = 3 * ATTN_WIDTH + 3 * CONV_WIDTH + 2 * D_MODEL
ATTN_SCALE = HEAD_DIM ** -0.5
ROPE_THETA = 10000.0
N_KEYS = 128
PEER_HEADS = 8
PEER_TOPK = 16
NORM_EPS = 1e-6
LOG2_E = 1.4426950408889634
N_BACK = 128
PAST_LEN = 8192

LANES = 128
SUBLANES = 8
VMEM_LIMIT = 52 * 1024 * 1024

TM_PROJ = 512
TM_MERGE = 512
TM_FINAL = 512
TK_ROUTE = 256
TM_FFN = 1024
EB_FFN = 1024
ATTN_BLK = 128
ATTN_STEP = 512
SATTN_SEQS = 2


def _unpack(packed):
    return pltpu.bitcast(packed, BF16)


def _pack(x):
    return pltpu.bitcast(x.astype(BF16), U32)


def _pack_kernel(w_ref, o_ref, *, transpose):
    w = w_ref[...]
    if transpose:
        w = w.T
    o_ref[...] = pltpu.bitcast(w.astype(BF16), U32)


def _pack_weight(w, transpose=False):
    r, c = w.shape
    tr = next(t for t in (1024, 512, 256, 128) if r % t == 0)
    tc = next(t for t in (1024, 512, 256, 128) if c % t == 0)
    if transpose:
        out_shape, out_spec = (c // 2, r), pl.BlockSpec((tc // 2, tr), lambda i, j: (j, i))
    else:
        out_shape, out_spec = (r // 2, c), pl.BlockSpec((tr // 2, tc), lambda i, j: (i, j))
    return pl.pallas_call(
        functools.partial(_pack_kernel, transpose=transpose),
        out_shape=jax.ShapeDtypeStruct(out_shape, U32),
        grid=(r // tr, c // tc),
        in_specs=[pl.BlockSpec((tr, tc), lambda i, j: (i, j))],
        out_specs=out_spec,
        compiler_params=_cparams(("arbitrary", "arbitrary")),
        name="pack_weight",
    )(w)


def _cparams(sem, flags=None):
    return pltpu.CompilerParams(dimension_semantics=sem, vmem_limit_bytes=VMEM_LIMIT, flags=flags)


def _ada_kernel(c_ref, w_ref, b_ref, o_ref):
    c = c_ref[...]
    a = (c * jax.nn.sigmoid(c)).astype(BF16)
    o_ref[...] = jnp.dot(a, w_ref[...].astype(BF16), preferred_element_type=F32) + b_ref[...]


def _ada(c_all, w_ada, b_ada):
    rows = c_all.shape[0]
    n_out = w_ada.shape[1]
    tn = 1024
    return pl.pallas_call(
        _ada_kernel,
        out_shape=jax.ShapeDtypeStruct((rows, n_out), F32),
        grid=(n_out // tn,),
        in_specs=[pl.BlockSpec((rows, D_MODEL), lambda j: (0, 0)),
                  pl.BlockSpec((D_MODEL, tn), lambda j: (0, j)),
                  pl.BlockSpec((1, tn), lambda j: (0, j))],
        out_specs=pl.BlockSpec((rows, tn), lambda j: (0, j)),
        compiler_params=_cparams(("arbitrary",)),
        name="adaln",
    )(c_all, w_ada, b_ada.reshape(1, n_out))


def _rmsnorm_mod(x, g, scale, shift):
    ms = jnp.mean(x * x, axis=-1, keepdims=True)
    y = x * lax.rsqrt(ms + NORM_EPS) * g
    return y * (1.0 + scale) + shift


def _inproj_kernel(x_ref, sc_ref, sh_ref, n1_ref, cos_ref, sa_ref, sb_ref, w_ref, *refs, by_residue):
    n_qkv = 3 * N_GROUPS if by_residue else 3
    bg_ref, u_ref, ga_ref, gc_ref, zbuf = refs[n_qkv:]
    n_chunks, tm = zbuf.shape[0], zbuf.shape[1]
    per_group = GROUP_WIDTH // LANES
    h = _rmsnorm_mod(x_ref[...], n1_ref[...], sc_ref[...], sh_ref[...]).astype(BF16)

    def proj(lo, hi):
        return jnp.dot(h, _unpack(w_ref[:, lo:hi]), preferred_element_type=F32)

    cos, sa, sb = cos_ref[...], sa_ref[...], sb_ref[...]

    def stage(z, rope, mult):
        for c in range(n_chunks):
            zc = z[:, c * LANES:(c + 1) * LANES]
            if rope:
                zc = zc * cos + pltpu.roll(zc, LANES - 32, 1) * sa + pltpu.roll(zc, 32, 1) * sb
            zbuf[c] = zc * mult if mult != 1.0 else zc

    def emit(which):
        for c in range(n_chunks):
            if not by_residue:
                refs[which][:, c * LANES:(c + 1) * LANES] = zbuf[c].astype(refs[which].dtype)
                continue
            g, cl = c // per_group, c % per_group
            ref, d = refs[which * N_GROUPS + g], DILATIONS[g]
            for r in range(d):
                ref[r, :, cl * LANES:(cl + 1) * LANES] = zbuf[c, pl.ds(r, tm // d, stride=d), :].astype(ref.dtype)

    o = 0
    stage(proj(o, o + ATTN_WIDTH), True, ATTN_SCALE)
    emit(0)
    o += ATTN_WIDTH
    stage(proj(o, o + ATTN_WIDTH), True, 1.0)
    emit(1)
    o += ATTN_WIDTH
    stage(proj(o, o + ATTN_WIDTH), False, 1.0)
    emit(2)
    o += ATTN_WIDTH
    bg_ref[...] = _pack(proj(o, o + CONV_WIDTH))
    o += CONV_WIDTH
    cg = proj(o, o + CONV_WIDTH)
    o += CONV_WIDTH
    u_ref[...] = cg * proj(o, o + CONV_WIDTH)
    o += CONV_WIDTH
    ga_ref[...] = _pack(proj(o, o + D_MODEL))
    o += D_MODEL
    gc_ref[...] = _pack(proj(o, o + D_MODEL))


def _inproj(x2d, mod_specs, mods, tab_spec, tabs, n1, w_in_p, residue_seq=None):
    n = x2d.shape[0]
    tm = TM_PROJ
    row = lambda w: pl.BlockSpec((tm, w), lambda i: (i, 0))
    const = lambda s: pl.BlockSpec(s, lambda i: (0, 0), pipeline_mode=pl.Buffered(1))
    if residue_seq is None:
        qkv_shapes = [jax.ShapeDtypeStruct((n, ATTN_WIDTH), dt) for dt in (BF16, F32, F32)]
        qkv_specs = [row(ATTN_WIDTH)] * 3
    else:
        bps = residue_seq // tm
        qkv_shapes, qkv_specs = [], []
        for dt in (BF16, F32, F32):
            for d in DILATIONS:
                qkv_shapes.append(jax.ShapeDtypeStruct((n // residue_seq, d, residue_seq // d, GROUP_WIDTH), dt))
                qkv_specs.append(pl.BlockSpec((None, d, tm // d, GROUP_WIDTH), lambda i: (i // bps, 0, i % bps, 0)))
    half = lambda w: pl.BlockSpec((tm // 2, w), lambda i: (i, 0))
    outs = qkv_shapes + [jax.ShapeDtypeStruct((n // 2, CONV_WIDTH), U32),
                         jax.ShapeDtypeStruct((n, CONV_WIDTH), F32),
                         jax.ShapeDtypeStruct((n // 2, D_MODEL), U32),
                         jax.ShapeDtypeStruct((n // 2, D_MODEL), U32)]
    return pl.pallas_call(
        functools.partial(_inproj_kernel, by_residue=residue_seq is not None),
        out_shape=outs,
        grid=(n // tm,),
        in_specs=[row(D_MODEL), mod_specs, mod_specs, const((1, D_MODEL)),
                  tab_spec, tab_spec, tab_spec, const(w_in_p.shape)],
        out_specs=qkv_specs + [half(CONV_WIDTH), row(CONV_WIDTH), half(D_MODEL), half(D_MODEL)],
        scratch_shapes=[pltpu.VMEM((ATTN_WIDTH // LANES, tm, LANES), F32)],
        compiler_params=_cparams(("arbitrary",)),
        name="inproj",
    )(x2d, mods[0], mods[1], n1, tabs[0], tabs[1], tabs[2], w_in_p)


def _pattn_kernel(q_ref, kp_ref, kc_ref, vp_ref, vc_ref, o_ref, l_ref):
    nb = pl.program_id(2)
    k = jnp.concatenate([kp_ref[...], kc_ref[...]], axis=0).astype(BF16)
    v = jnp.concatenate([vp_ref[...], vc_ref[...]], axis=0).astype(BF16)
    qi = lax.broadcasted_iota(jnp.int32, (ATTN_BLK, 2 * ATTN_BLK), 0)
    ki = lax.broadcasted_iota(jnp.int32, (ATTN_BLK, 2 * ATTN_BLK), 1)
    dist = qi + ATTN_BLK - ki
    band = (dist >= 0) & (dist <= N_BACK)
    band0 = band & (ki >= jnp.where(nb > 0, 0, ATTN_BLK))
    for sb in range(ATTN_STEP // ATTN_BLK):
        rows = slice(sb * ATTN_BLK, (sb + 1) * ATTN_BLK)
        keys = slice(sb * ATTN_BLK, (sb + 2) * ATTN_BLK)
        q = q_ref[rows, :]
        valid = band0 if sb == 0 else band
        for h in range(HEADS_PER_GROUP):
            sl = slice(h * HEAD_DIM, (h + 1) * HEAD_DIM)
            s = lax.dot_general(q[:, sl], k[keys, sl], (((1,), (1,)), ((), ())), preferred_element_type=F32)
            s = jnp.where(valid, s, -jnp.inf)
            m = jnp.max(s, axis=-1, keepdims=True)
            p = jnp.exp(s - m)
            den = jnp.sum(p, axis=-1, keepdims=True)
            o = jnp.dot(p.astype(BF16), v[keys, sl], preferred_element_type=F32) / den
            o_ref[rows, sl] = o
            l_ref[rows, sl] = jnp.broadcast_to(m + jnp.log(den), (ATTN_BLK, HEAD_DIM))


def _pattn(q, k, v, g):
    batch, d, n, _ = q.shape
    per_blk = ATTN_STEP // ATTN_BLK
    cur = pl.BlockSpec((None, None, ATTN_STEP, GROUP_WIDTH), lambda b, r, nb: (b, r, nb, 0))
    prev = pl.BlockSpec((None, None, ATTN_BLK, GROUP_WIDTH),
                        lambda b, r, nb: (b, r, jnp.maximum(nb * per_blk - 1, 0), 0))
    return pl.pallas_call(
        _pattn_kernel,
        out_shape=[jax.ShapeDtypeStruct((batch, d, n, GROUP_WIDTH), F32)] * 2,
        grid=(batch, d, n // ATTN_STEP),
        in_specs=[cur, prev, cur, prev, cur],
        out_specs=[cur, cur],
        compiler_params=_cparams(("arbitrary",) * 3),
        name=f"pattn{g}",
    )(q, k, k, v, v)


def _sattn_kernel(q_ref, k_ref, v_ref, c0_ref, c1_ref, c2_ref, o_ref, *, s_len):
    for s in range(q_ref.shape[0]):
        _sattn_one(q_ref.at[s], k_ref.at[s], v_ref.at[s], c0_ref.at[s], c1_ref.at[s], c2_ref.at[s], o_ref.at[s],
                   s_len=s_len)


def _sattn_one(q_ref, k_ref, v_ref, c0_ref, c1_ref, c2_ref, o_ref, *, s_len):
    q = q_ref[...].astype(F32)
    kn = k_ref[...]
    vn = v_ref[...]
    rows = HEADS_PER_GROUP * s_len
    lane_head = lax.broadcasted_iota(jnp.int32, (s_len, GROUP_WIDTH), 1) // HEAD_DIM
    pad = jnp.zeros((LANES - s_len, GROUP_WIDTH), F32)
    nt_dims = (((1,), (1,)), ((), ()))
    ms, ls, os_ = [], [], []
    for g, cref in enumerate((c0_ref, c1_ref, c2_ref)):
        d, win = DILATIONS[g], WINDOWS[g]
        gs = slice(g * GROUP_WIDTH, (g + 1) * GROUP_WIDTH)
        qg = q[:, gs]
        qexp = jnp.concatenate(
            [jnp.where(lane_head == h, qg, 0.0) for h in range(HEADS_PER_GROUP)], axis=0).astype(BF16)
        s_c = jnp.dot(qexp, cref[0].astype(BF16), preferred_element_type=F32)
        s_n = lax.dot_general(qexp, jnp.concatenate([kn[:, gs], pad], axis=0).astype(BF16), nt_dims,
                              preferred_element_type=F32)
        back_c = (win + lax.broadcasted_iota(jnp.int32, (rows, win), 0) % s_len
                  - lax.broadcasted_iota(jnp.int32, (rows, win), 1))
        col_n = lax.broadcasted_iota(jnp.int32, (rows, LANES), 1)
        back_n = lax.broadcasted_iota(jnp.int32, (rows, LANES), 0) % s_len - col_n
        s_c = jnp.where((back_c <= win) & ((back_c & (d - 1)) == 0), s_c, -jnp.inf)
        s_n = jnp.where((back_n >= 0) & ((back_n & (d - 1)) == 0) & (col_n < s_len), s_n, -jnp.inf)
        m = jnp.maximum(jnp.max(s_c, axis=-1, keepdims=True), jnp.max(s_n, axis=-1, keepdims=True))
        p_c = jnp.exp(s_c - m)
        p_n = jnp.exp(s_n - m)
        ls.append(jnp.sum(p_c, axis=-1, keepdims=True) + jnp.sum(p_n, axis=-1, keepdims=True))
        ms.append(m)
        o = lax.dot_general(p_c.astype(BF16), cref[1].astype(BF16), nt_dims, preferred_element_type=F32)
        o = o + jnp.dot(p_n.astype(BF16), jnp.concatenate([vn[:, gs], pad], axis=0).astype(BF16),
                        preferred_element_type=F32)
        os_.append(o)
    mm = jnp.maximum(jnp.maximum(ms[0], ms[1]), ms[2])
    num = jnp.zeros((rows, GROUP_WIDTH), F32)
    den = jnp.zeros((rows, 1), F32)
    for g in range(N_GROUPS):
        w = jnp.exp(ms[g] - mm)
        num = num + w * os_[g]
        den = den + w * ls[g]
    full = num / den
    out = jnp.zeros((s_len, GROUP_WIDTH), F32)
    for h in range(HEADS_PER_GROUP):
        out = out + jnp.where(lane_head == h, full[h * s_len:(h + 1) * s_len], 0.0)
    o_ref[...] = out


def _sattn(q, k, v, caches, n_seq, s_len):
    q3 = q.reshape(n_seq, s_len, ATTN_WIDTH)
    k3 = k.reshape(n_seq, s_len, ATTN_WIDTH)
    v3 = v.reshape(n_seq, s_len, ATTN_WIDTH)
    cts = [jnp.transpose(c, (0, 2, 3, 4, 1)).reshape(n_seq, 2, GROUP_WIDTH, c.shape[1]) for c in caches]
    ns = SATTN_SEQS
    tok = pl.BlockSpec((ns, s_len, ATTN_WIDTH), lambda b: (b, 0, 0))
    out = pl.pallas_call(
        functools.partial(_sattn_kernel, s_len=s_len),
        out_shape=jax.ShapeDtypeStruct((n_seq, s_len, GROUP_WIDTH), F32),
        grid=(n_seq // ns,),
        in_specs=[tok, tok, tok] + [pl.BlockSpec((ns, 2, GROUP_WIDTH, w), lambda b: (b, 0, 0, 0)) for w in WINDOWS],
        out_specs=pl.BlockSpec((ns, s_len, GROUP_WIDTH), lambda b: (b, 0, 0)),
        compiler_params=_cparams(("arbitrary",)),
        name="sattn",
    )(q3, k3, v3, *cts)
    return out.reshape(n_seq * s_len, GROUP_WIDTH)


def _merge_tail(o_attn, bg, yc, ga_ref, gc_ref, x_ref, g1_ref, n2_ref, sc2_ref, sh2_ref,
                wa_ref, wc_ref, wo_ref, x1_ref, h2t_ref):
    a_out = jnp.dot(o_attn.astype(BF16), _unpack(wa_ref[...]), preferred_element_type=F32)
    c_out = jnp.dot((bg * yc).astype(BF16), _unpack(wc_ref[...]), preferred_element_type=F32)
    ga, gc = _unpack(ga_ref[...]).astype(F32), _unpack(gc_ref[...]).astype(F32)
    mix = jax.nn.sigmoid(ga) * a_out + jax.nn.sigmoid(gc) * c_out
    mo = jnp.dot(mix.astype(BF16), _unpack(wo_ref[...]), preferred_element_type=F32)
    x1 = x_ref[...] + g1_ref[...] * mo
    x1_ref[...] = x1
    h2 = _rmsnorm_mod(x1, n2_ref[...], sc2_ref[...], sh2_ref[...])
    h2t_ref[...] = pltpu.bitcast(h2.T.astype(BF16), U32)


def _conv3(u, um1, um2, cw_ref):
    return cw_ref[0:1, :] * um2 + cw_ref[1:2, :] * um1 + cw_ref[2:3, :] * u


def _merge_prompt_kernel(o0_ref, o1_ref, o2_ref, l0_ref, l1_ref, l2_ref, bg_ref, u_ref, uh_ref, cw_ref,
                         ga_ref, gc_ref, x_ref, g1_ref, n2_ref, sc2_ref, sh2_ref, wa_ref, wc_ref, wo_ref,
                         x1_ref, h2t_ref, ubuf, *rowbufs, blocks_per_seq):
    tm = u_ref.shape[0]
    first = (pl.program_id(0) % blocks_per_seq) == 0

    def by_position(ref, buf):
        d, rows = ref.shape[0], ref.shape[1]
        if d == 1:
            return ref[0]
        for r in range(d):
            for c in range(GROUP_WIDTH // LANES):
                buf[c, pl.ds(r, rows, stride=d), :] = ref[r, :, c * LANES:(c + 1) * LANES]
        return jnp.concatenate([buf[c] for c in range(GROUP_WIDTH // LANES)], axis=1)

    o0, l0 = by_position(o0_ref, None), by_position(l0_ref, None)
    o1, l1 = by_position(o1_ref, rowbufs[0]), by_position(l1_ref, rowbufs[1])
    o2, l2 = by_position(o2_ref, rowbufs[2]), by_position(l2_ref, rowbufs[3])
    mm = jnp.maximum(jnp.maximum(l0, l1), l2)
    e0, e1, e2 = jnp.exp(l0 - mm), jnp.exp(l1 - mm), jnp.exp(l2 - mm)
    o_attn = (e0 * o0 + e1 * o1 + e2 * o2) / (e0 + e1 + e2)
    u = u_ref[...]
    ubuf[0:SUBLANES, :] = jnp.where(first, 0.0, uh_ref[...])
    ubuf[SUBLANES:SUBLANES + tm, :] = u
    yc = _conv3(u, ubuf[SUBLANES - 1:SUBLANES - 1 + tm, :], ubuf[SUBLANES - 2:SUBLANES - 2 + tm, :], cw_ref)
    _merge_tail(o_attn, _unpack(bg_ref[...]).astype(F32), yc, ga_ref, gc_ref, x_ref, g1_ref, n2_ref, sc2_ref, sh2_ref,
                wa_ref, wc_ref, wo_ref, x1_ref, h2t_ref)


def _merge_sample_kernel(oa_ref, bg_ref, u_ref, p1_ref, p2_ref, cw_ref,
                         ga_ref, gc_ref, x_ref, g1_ref, n2_ref, sc2_ref, sh2_ref, wa_ref, wc_ref, wo_ref,
                         h2t_all_ref, x1_ref, h2t_ref, ubuf, *, s_len):
    del h2t_all_ref
    tm = u_ref.shape[0]
    u = u_ref[...]
    ubuf[0:SUBLANES, :] = jnp.zeros((SUBLANES, CONV_WIDTH), F32)
    ubuf[SUBLANES:SUBLANES + tm, :] = u
    t = lax.broadcasted_iota(jnp.int32, (tm, CONV_WIDTH), 0) % s_len
    um1 = jnp.where(t < 1, p1_ref[...], ubuf[SUBLANES - 1:SUBLANES - 1 + tm, :])
    um2 = jnp.where(t < 2, p2_ref[...], ubuf[SUBLANES - 2:SUBLANES - 2 + tm, :])
    yc = _conv3(u, um1, um2, cw_ref)
    _merge_tail(oa_ref[...], _unpack(bg_ref[...]).astype(F32), yc, ga_ref, gc_ref, x_ref, g1_ref, n2_ref, sc2_ref, sh2_ref,
                wa_ref, wc_ref, wo_ref, x1_ref, h2t_ref)


def _merge_common_specs(tm, mod_spec):
    row = lambda w: pl.BlockSpec((tm, w), lambda i: (i, 0))
    const = lambda s: pl.BlockSpec(s, lambda i: (0, 0), pipeline_mode=pl.Buffered(1))
    half = lambda w: pl.BlockSpec((tm // 2, w), lambda i: (i, 0))
    ins = [half(D_MODEL), half(D_MODEL), row(D_MODEL), mod_spec, const((1, D_MODEL)), mod_spec, mod_spec,
           const((GROUP_WIDTH // 2, D_MODEL)), const((CONV_WIDTH // 2, D_MODEL)), const((D_MODEL // 2, D_MODEL))]
    outs = [row(D_MODEL), pl.BlockSpec((D_MODEL // 2, tm), lambda i: (0, i))]
    return ins, outs


def _merge_prompt(attn, bg, u, conv_w, ga, gc, x2d, mod_spec, g1, n2, sc2, sh2, wa, wc, wo, seq, n_total):
    n = x2d.shape[0]
    tm = TM_MERGE
    row = lambda w: pl.BlockSpec((tm, w), lambda i: (i, 0))
    halo = pl.BlockSpec((SUBLANES, CONV_WIDTH), lambda i: (jnp.maximum(i * (tm // SUBLANES) - 1, 0), 0))
    ins, outs = _merge_common_specs(tm, mod_spec)
    (o0, l0), (o1, l1), (o2, l2) = attn
    bps = seq // tm
    grp = [pl.BlockSpec((None, d, tm // d, GROUP_WIDTH), lambda i: (i // bps, 0, i % bps, 0)) for d in DILATIONS]
    return pl.pallas_call(
        functools.partial(_merge_prompt_kernel, blocks_per_seq=bps),
        out_shape=[jax.ShapeDtypeStruct((n, D_MODEL), F32), jax.ShapeDtypeStruct((D_MODEL // 2, n_total), U32)],
        grid=(n // tm,),
        in_specs=grp + grp + [pl.BlockSpec((tm // 2, CONV_WIDTH), lambda i: (i, 0)), row(CONV_WIDTH), halo,
                              pl.BlockSpec((CONV_K, CONV_WIDTH), lambda i: (0, 0))] + ins,
        out_specs=outs,
        scratch_shapes=[pltpu.VMEM((SUBLANES + tm, CONV_WIDTH), F32)] + [pltpu.VMEM((GROUP_WIDTH // LANES, tm, LANES), F32)] * 4,
        compiler_params=_cparams(("arbitrary",)),
        name="merge_prompt",
    )(o0, o1, o2, l0, l1, l2, bg, u, u, conv_w, ga, gc, x2d, g1, n2, sc2, sh2, wa, wc, wo)


def _merge_sample(o_attn, bg, u, p1, p2, conv_w, ga, gc, x2d, mod_spec, g1, n2, sc2, sh2, wa, wc, wo, s_len,
                  h2t_all, col_off):
    n = x2d.shape[0]
    tm = TM_MERGE
    row = lambda w: pl.BlockSpec((tm, w), lambda i: (i, 0))
    ins, outs = _merge_common_specs(tm, mod_spec)
    outs = [outs[0], pl.BlockSpec((D_MODEL // 2, tm), lambda i: (0, i + col_off // tm))]
    in_specs = ([row(GROUP_WIDTH), pl.BlockSpec((tm // 2, CONV_WIDTH), lambda i: (i, 0))] + [row(CONV_WIDTH)] * 3
                + [pl.BlockSpec((CONV_K, CONV_WIDTH), lambda i: (0, 0))] + ins + [pl.BlockSpec(memory_space=pl.ANY)])
    return pl.pallas_call(
        functools.partial(_merge_sample_kernel, s_len=s_len),
        out_shape=[jax.ShapeDtypeStruct((n, D_MODEL), F32), jax.ShapeDtypeStruct(h2t_all.shape, U32)],
        grid=(n // tm,),
        in_specs=in_specs,
        out_specs=outs,
        input_output_aliases={len(in_specs) - 1: 1},
        scratch_shapes=[pltpu.VMEM((SUBLANES + tm, CONV_WIDTH), F32)],
        compiler_params=_cparams(("arbitrary",)),
        name="merge_sample",
    )(o_attn, bg, u, p1, p2, conv_w, ga, gc, x2d, g1, n2, sc2, sh2, wa, wc, wo, h2t_all)


def _oddeven_merge_sort_pairs(n):
    pairs = []
    p = 1
    while p < n:
        k = p
        while k >= 1:
            for j in range(k % p, n - k, 2 * k):
                for i in range(min(k, n - j - k)):
                    if (i + j) // (2 * p) == (i + j + k) // (2 * p):
                        pairs.append((i + j, i + j + k))
            k //= 2
        p *= 2
    return pairs


_SORT16 = _oddeven_merge_sort_pairs(PEER_TOPK)


def _cmpx(x, a, b):
    hi, lo = jnp.maximum(x[a], x[b]), jnp.minimum(x[a], x[b])
    x[a], x[b] = hi, lo


def _bitonic_clean(x):
    stride = PEER_TOPK // 2
    while stride >= 1:
        for i in range(PEER_TOPK):
            if i & stride == 0:
                _cmpx(x, i, i + stride)
        stride //= 2
    return x


def _merge_top16(a, b):
    c = []
    for j in range(PEER_TOPK):
        jb = PEER_TOPK - 1 - j
        c.append(jnp.maximum(a[j], b[jb]) if jb < len(b) else a[j])
    return _bitonic_clean(c)


def _top16_over_keys(s):
    x = [s[j * SUBLANES:(j + 1) * SUBLANES, :] for j in range(N_KEYS // SUBLANES)]
    for a, b in _SORT16:
        _cmpx(x, a, b)
    shift = SUBLANES // 2
    while shift >= 1:
        x = _merge_top16(x, [pltpu.roll(v, shift, 0) for v in x])
        shift //= 2
    return x


def _route_kernel(h_ref, wq_ref, k1_ref, k2_ref, cnt_ref, m1_ref, r2_ref, e2_ref):
    tk = h_ref.shape[1]
    qt = jnp.dot(_unpack(wq_ref[...]), _unpack(h_ref[...]), preferred_element_type=F32).astype(BF16)
    k1, k2 = k1_ref[...], k2_ref[...]
    sub = lax.broadcasted_iota(jnp.int32, (SUBLANES, tk), 0)
    half = N_KEYS
    s1s, s2s = [], []
    v1 = [jnp.zeros((SUBLANES, tk), F32) for _ in range(PEER_TOPK)]
    v2 = [jnp.zeros((SUBLANES, tk), F32) for _ in range(PEER_TOPK)]
    for h in range(PEER_HEADS):
        base = h * 2 * half
        s1 = jnp.dot(k1, qt[base:base + half], preferred_element_type=F32)
        s2 = jnp.dot(k2, qt[base + half:base + 2 * half], preferred_element_type=F32)
        s1s.append(s1)
        s2s.append(s2)
        t1 = _top16_over_keys(s1)
        t2 = _top16_over_keys(s2)
        for j in range(PEER_TOPK):
            v1[j] = jnp.where(sub == h, t1[j], v1[j])
            v2[j] = jnp.where(sub == h, t2[j], v2[j])
    psum = {}
    for a in range(PEER_TOPK):
        for b in range(PEER_TOPK // (a + 1)):
            psum[a, b] = v1[a] + v2[b]
    lists = [[psum[a, b] for b in range(PEER_TOPK // (a + 1))] for a in range(PEER_TOPK // 2)]
    lists.append([psum[a, 0] for a in range(PEER_TOPK // 2, PEER_TOPK)])
    top = lists[0]
    for other in lists[1:]:
        top = _merge_top16(top, other)
    tau = top[PEER_TOPK - 1]
    z = jnp.ones((SUBLANES, tk), F32)
    for j in range(1, PEER_TOPK):
        z = z + jnp.exp(top[j] - top[0])
    rz = 1.0 / z
    x1 = []
    for b in range(PEER_TOPK):
        x = jnp.full((SUBLANES, tk), jnp.inf, F32)
        for a in range(PEER_TOPK // (b + 1)):
            x = jnp.where(psum[a, b] >= tau, v1[a], x)
        x1.append(x)
    for h in range(PEER_HEADS):
        s1, s2 = s1s[h], s2s[h]
        row = lambda v: v[h:h + 1, :]
        cnt = jnp.zeros_like(s1)
        rank = jnp.full_like(s2, float(PEER_TOPK))
        for j in range(PEER_TOPK):
            cnt = jnp.where(s1 >= row(x1[j]), float(j + 1), cnt)
        for j in range(PEER_TOPK - 1, -1, -1):
            rank = jnp.where(s2 >= row(v2[j]), float(j), rank)
        m1 = jnp.exp(s1 - row(v1[0])) * row(rz)
        cnt_ref[:, h, :, :] = cnt.reshape(N_KEYS // SUBLANES, SUBLANES, tk)
        m1_ref[:, h, :, :] = m1.reshape(N_KEYS // SUBLANES, SUBLANES, tk)
        half_rows = slice(h * N_KEYS // 2, (h + 1) * N_KEYS // 2)
        r2_ref[half_rows, :] = pltpu.bitcast(rank.astype(BF16), U32)
        e2_ref[half_rows, :] = pltpu.bitcast(jnp.exp(s2 - row(v2[0])).astype(BF16), U32)


def _route(h2t, wq_t, k1, k2):
    n = h2t.shape[1]
    tk = TK_ROUTE
    a8 = N_KEYS // SUBLANES
    s14 = jax.ShapeDtypeStruct((a8, PEER_HEADS, SUBLANES, n), F32)
    s2d = jax.ShapeDtypeStruct((PEER_HEADS * N_KEYS // 2, n), U32)
    spec4 = pl.BlockSpec((a8, PEER_HEADS, SUBLANES, tk), lambda i: (0, 0, 0, i))
    spec2 = pl.BlockSpec((PEER_HEADS * N_KEYS // 2, tk), lambda i: (0, i))
    return pl.pallas_call(
        _route_kernel,
        out_shape=[s14, s14, s2d, s2d],
        grid=(n // tk,),
        in_specs=[pl.BlockSpec((D_MODEL // 2, tk), lambda i: (0, i)),
                  pl.BlockSpec(wq_t.shape, lambda i: (0, 0)),
                  pl.BlockSpec(k1.shape, lambda i: (0, 0)),
                  pl.BlockSpec(k2.shape, lambda i: (0, 0))],
        out_specs=[spec4, spec4, spec2, spec2],
        compiler_params=_cparams(("arbitrary",)),
        name="peer_route",
    )(h2t, wq_t, k1, k2)


FFN_MXU_TILE = 256
FFN_VPU_TILE = LANES
BF16_ROWS = 16
FFN_ROW_CHUNK = 64


def _ffn_kernel(h_ref, u_ref, vt_ref, cnt_ref, m1_ref, r2_ref, e2_ref, o_ref, coef_ref, *act_refs):
    tm = h_ref.shape[1]
    n_bt = N_KEYS // BF16_ROWS

    def row16(ref, h, al, ls):
        row = ref[al // SUBLANES, h, al % SUBLANES:al % SUBLANES + 1, ls]
        return jnp.broadcast_to(row, (BF16_ROWS, FFN_VPU_TILE)).astype(BF16)

    def mxu_cols(mt):
        return slice(mt * FFN_MXU_TILE, (mt + 1) * FFN_MXU_TILE)

    def build_gates(mt):
        for vt in range(FFN_MXU_TILE // FFN_VPU_TILE):
            lo = mt * FFN_MXU_TILE + vt * FFN_VPU_TILE
            ls = slice(lo, lo + FFN_VPU_TILE)
            for a0 in range(0, EB_FFN // N_KEYS, 2):
                acc = [[jnp.zeros((BF16_ROWS, FFN_VPU_TILE), BF16) for _ in range(n_bt)] for _ in range(2)]
                for h in range(PEER_HEADS):
                    cnts = [row16(cnt_ref, h, a0 + k, ls) for k in range(2)]
                    m1s = [row16(m1_ref, h, a0 + k, ls) for k in range(2)]
                    for bt in range(n_bt):
                        rs = slice((h * N_KEYS + bt * BF16_ROWS) // 2, (h * N_KEYS + (bt + 1) * BF16_ROWS) // 2)
                        rank, e2 = _unpack(r2_ref[rs, ls]), _unpack(e2_ref[rs, ls])
                        for k in range(2):
                            acc[k][bt] = acc[k][bt] + jnp.where(rank < cnts[k], e2, jnp.zeros_like(e2)) * m1s[k]
                for k in range(2):
                    for bt in range(n_bt):
                        es = slice((a0 + k) * N_KEYS + bt * BF16_ROWS, (a0 + k) * N_KEYS + (bt + 1) * BF16_ROWS)
                        coef_ref[es, ls] = acc[k][bt]

    def up_proj(mt):
        act_refs[mt % 2][...] = jnp.dot(_unpack(u_ref[...]), _unpack(h_ref[:, mxu_cols(mt)]),
                                        preferred_element_type=F32)

    def down_proj(mt):
        ms = mxu_cols(mt)
        k0 = (2.0 / jnp.pi) ** 0.5
        zero = jnp.minimum(pl.program_id(1), 0)
        for rc in range(EB_FFN // FFN_ROW_CHUNK):
            rs = slice(rc * FFN_ROW_CHUNK, (rc + 1) * FFN_ROW_CHUNK)
            x = act_refs[mt % 2][pl.ds(pl.multiple_of(zero + rc * FFN_ROW_CHUNK, FFN_ROW_CHUNK), FFN_ROW_CHUNK), :]
            c0 = -2.0 * k0 * LOG2_E
            gelu = x / (1.0 + jnp.exp2(x * (c0 + (c0 * 0.044715) * (x * x))))
            coef_ref[rs, ms] = coef_ref[rs, ms] * gelu.astype(BF16)
        o_ref[:, ms] += jnp.dot(_unpack(vt_ref[...]), coef_ref[:, ms], preferred_element_type=F32)

    @pl.when(pl.program_id(1) == 0)
    def _():
        o_ref[...] = jnp.zeros_like(o_ref)

    n_mt = tm // FFN_MXU_TILE
    build_gates(0)
    up_proj(0)
    for mt in range(n_mt):
        if mt + 1 < n_mt:
            build_gates(mt + 1)
            up_proj(mt + 1)
        down_proj(mt)


def _ffn(h2t, u_b, vt_b, cnt, m1, r2, e2):
    n = h2t.shape[1]
    tm, eb = TM_FFN, EB_FFN
    n_exp = 2 * u_b.shape[0]
    spec4 = pl.BlockSpec((eb // (N_KEYS * SUBLANES), PEER_HEADS, SUBLANES, tm), lambda i, e: (e, 0, 0, i))
    spec2 = pl.BlockSpec((PEER_HEADS * N_KEYS // 2, tm), lambda i, e: (0, i))
    return pl.pallas_call(
        _ffn_kernel,
        out_shape=jax.ShapeDtypeStruct((D_MODEL, n), F32),
        grid=(n // tm, n_exp // eb),
        in_specs=[pl.BlockSpec((D_MODEL // 2, tm), lambda i, e: (0, i)),
                  pl.BlockSpec((eb // 2, D_MODEL), lambda i, e: (e, 0)),
                  pl.BlockSpec((D_MODEL // 2, eb), lambda i, e: (0, e)),
                  spec4, spec4, spec2, spec2],
        out_specs=pl.BlockSpec((D_MODEL, tm), lambda i, e: (0, i)),
        scratch_shapes=[pltpu.VMEM((eb, tm), BF16)] + [pltpu.VMEM((eb, FFN_MXU_TILE), F32)] * 2,
        compiler_params=_cparams(("arbitrary", "arbitrary")),
        name="peer_ffn",
    )(h2t, u_b, vt_b, cnt, m1, r2, e2)


def _final_kernel(p_ref, x_ref, g2_ref, nf_ref, y_ref):
    x2 = x_ref[...] + g2_ref[...] * p_ref[...].T
    ms = jnp.mean(x2 * x2, axis=-1, keepdims=True)
    y_ref[...] = x2 * lax.rsqrt(ms + NORM_EPS) * nf_ref[...]


def _final(peer_t, col_off, x1, mod_spec, g2, nf):
    n = x1.shape[0]
    tm = TM_FINAL
    off = col_off // tm
    return pl.pallas_call(
        _final_kernel,
        out_shape=jax.ShapeDtypeStruct((n, D_MODEL), F32),
        grid=(n // tm,),
        in_specs=[pl.BlockSpec((D_MODEL, tm), lambda i: (0, i + off)),
                  pl.BlockSpec((tm, D_MODEL), lambda i: (i, 0)),
                  mod_spec,
                  pl.BlockSpec((1, D_MODEL), lambda i: (0, 0))],
        out_specs=pl.BlockSpec((tm, D_MODEL), lambda i: (i, 0)),
        compiler_params=_cparams(("arbitrary",)),
        name="final",
    )(peer_t, x1, g2, nf)


def _rope_tables(pos):
    half = HEAD_DIM // 2
    inv = ROPE_THETA ** (-jnp.arange(half, dtype=F32) / half)
    ang = pos.astype(F32)[:, None] * inv[None, :]
    cos, sin = jnp.cos(ang), jnp.sin(ang)
    zero = jnp.zeros_like(sin)
    reps = LANES // HEAD_DIM
    cos_t = jnp.tile(jnp.concatenate([cos, cos], axis=1), (1, reps))
    sa_t = jnp.tile(jnp.concatenate([-sin, zero], axis=1), (1, reps))
    sb_t = jnp.tile(jnp.concatenate([zero, sin], axis=1), (1, reps))
    return cos_t, sa_t, sb_t


def kernel(x_prompt, x_sample, cache_kv0, cache_kv1, cache_kv2, state_conv, c_prompt, c_sample,
           norm1_g, norm2_g, norm_f_g, w_ada, b_ada, w_in, conv_w, w_attn_o, w_conv_o, w_o,
           w_query, sub_keys, expert_u, expert_v):
    batch, seq, _ = x_prompt.shape
    n_seq, s_len, _ = x_sample.shape
    depth = w_in.shape[0]
    assert depth == 1
    assert tuple(c.shape[2] for c in (cache_kv0, cache_kv1, cache_kv2)) == WINDOWS
    assert s_len == SUBLANES and seq % (ATTN_STEP * DILATIONS[2]) == 0
    n_p, n_s = batch * seq, n_seq * s_len
    l = 0

    w_in_b = _pack_weight(w_in[l])
    wa_b, wc_b, wo_b = _pack_weight(w_attn_o[l]), _pack_weight(w_conv_o[l]), _pack_weight(w_o[l])
    wq_t = _pack_weight(w_query[l], transpose=True)
    k1_b, k2_b = sub_keys[l, 0].astype(BF16), sub_keys[l, 1].astype(BF16)
    u_b = _pack_weight(expert_u[l])
    vt_b = _pack_weight(expert_v[l], transpose=True)
    n1, n2, nf = norm1_g[l][None, :], norm2_g[l][None, :], norm_f_g[None, :]

    n_c = batch + n_seq
    c_all = jnp.concatenate([c_prompt, c_sample, jnp.zeros((-n_c % SUBLANES, D_MODEL), F32)], axis=0)
    mod = _ada(c_all, w_ada[l], b_ada[l])
    mod_p = [m[:, None, :] for m in jnp.split(mod[:batch], 6, axis=-1)]
    mod_s = jnp.split(jnp.repeat(mod[batch:n_c], s_len, axis=0), 6, axis=-1)

    def seq_spec(tm):
        return pl.BlockSpec((None, 1, D_MODEL), lambda i: (i // (seq // tm), 0, 0))

    def tok_spec(tm):
        return pl.BlockSpec((tm, D_MODEL), lambda i: (i, 0))

    tabs_p = _rope_tables(jnp.arange(seq))
    tabs_s = [jnp.tile(t, (n_seq, 1)) for t in _rope_tables(PAST_LEN + jnp.arange(s_len))]
    tab_p_spec = pl.BlockSpec((TM_PROJ, LANES), lambda i: (i % (seq // TM_PROJ), 0))
    tab_s_spec = pl.BlockSpec((TM_PROJ, LANES), lambda i: (i, 0))

    xp = x_prompt.reshape(n_p, D_MODEL)
    xs = x_sample.reshape(n_s, D_MODEL)
    *qkv_p, bgp, up, gap, gcp = _inproj(xp, seq_spec(TM_PROJ), (mod_p[1], mod_p[0]), tab_p_spec, tabs_p, n1, w_in_b,
                                        residue_seq=seq)
    qp, kp, vp = qkv_p[0:N_GROUPS], qkv_p[N_GROUPS:2 * N_GROUPS], qkv_p[2 * N_GROUPS:]
    qs, ks, vs, bgs, us, gas, gcs = _inproj(xs, tok_spec(TM_PROJ), (mod_s[1], mod_s[0]), tab_s_spec, tabs_s, n1, w_in_b)

    attn_p = [_pattn(qp[g], kp[g], vp[g], g) for g in range(N_GROUPS)]
    caches = (cache_kv0[l], cache_kv1[l], cache_kv2[l])
    attn_s = _sattn(qs, ks, vs, caches, n_seq, s_len)

    x1p, h2tp = _merge_prompt(attn_p, bgp, up, conv_w[l], gap, gcp, xp, seq_spec(TM_MERGE),
                              mod_p[2], n2, mod_p[4], mod_p[3], wa_b, wc_b, wo_b, seq, n_p + n_s)
    st = state_conv[l]
    p1 = jnp.pad(st[:, 1:2], ((0, 0), (0, s_len - 1), (0, 0))).reshape(n_s, CONV_WIDTH)
    p2 = jnp.pad(st, ((0, 0), (0, s_len - 2), (0, 0))).reshape(n_s, CONV_WIDTH)
    x1s, h2t = _merge_sample(attn_s, bgs, us, p1, p2, conv_w[l], gas, gcs, xs, tok_spec(TM_MERGE),
                             mod_s[2], n2, mod_s[4], mod_s[3], wa_b, wc_b, wo_b, s_len, h2tp, n_p)
    cnt, m1, r2, e2 = _route(h2t, wq_t, k1_b, k2_b)
    peer_t = _ffn(h2t, u_b, vt_b, cnt, m1, r2, e2)

    yp = _final(peer_t, 0, x1p, seq_spec(TM_FINAL), mod_p[5], nf).reshape(batch, seq, D_MODEL)
    ys = _final(peer_t, n_p, x1s, tok_spec(TM_FINAL), mod_s[5], nf).reshape(n_seq, s_len, D_MODEL)

    def prompt_tail(a, g):
        d, n_rows = a.shape[1], a.shape[2]
        t = jnp.swapaxes(a[:, :, n_rows - N_BACK:, :], 1, 2)
        return t.reshape(batch, N_BACK * d, HEADS_PER_GROUP, HEAD_DIM)

    kv_p, kv_s = [], []
    for g in range(N_GROUPS):
        assert WINDOWS[g] == N_BACK * DILATIONS[g] <= seq
        gs = slice(g * GROUP_WIDTH, (g + 1) * GROUP_WIDTH)
        kv_p.append(jnp.stack([prompt_tail(kp[g], g), prompt_tail(vp[g], g)], axis=2)[None])
        k4 = ks.reshape(n_seq, s_len, ATTN_WIDTH)[:, :, gs].reshape(n_seq, s_len, HEADS_PER_GROUP, HEAD_DIM)
        v4 = vs.reshape(n_seq, s_len, ATTN_WIDTH)[:, :, gs].reshape(n_seq, s_len, HEADS_PER_GROUP, HEAD_DIM)
        kv_s.append(jnp.stack([k4, v4], axis=2)[None])
    conv_p = up.reshape(batch, seq, CONV_WIDTH)[:, seq - (CONV_K - 1):][None]
    conv_s = jnp.concatenate([st, us.reshape(n_seq, s_len, CONV_WIDTH)], axis=1)[:, s_len:][None]
    return (yp, ys, kv_p[0], kv_p[1], kv_p[2], conv_p, kv_s[0], kv_s[1], kv_s[2], conv_s)
```

```python
import functools

import jax
import jax.numpy as jnp
from jax import lax
from jax.experimental import pallas as pl
from jax.experimental.pallas import tpu as pltpu

F32 = jnp.float32
BF16 = jnp.bfloat16
U32 = jnp.uint32

D_MODEL = 1024
HEAD_DIM = 64
HEADS_PER_GROUP = 4
WINDOWS = (128, 512, 2048)
DILATIONS = (1, 4, 16)
N_GROUPS = 3
GROUP_WIDTH = HEADS_PER_GROUP * HEAD_DIM
ATTN_WIDTH = N_GROUPS * GROUP_WIDTH
CONV_WIDTH = 768
CONV_K = 3
IN_WIDTH = 3 * ATTN_WIDTH + 3 * CONV_WIDTH + 2 * D_MODEL
ATTN_SCALE = HEAD_DIM ** -0.5
ROPE_THETA = 10000.0
N_KEYS = 128
PEER_HEADS = 8
PEER_TOPK = 16
NORM_EPS = 1e-6
LOG2_E = 1.4426950408889634
N_BACK = 128
PAST_LEN = 8192

LANES = 128
SUBLANES = 8
VMEM_LIMIT = 52 * 1024 * 1024

TM_PROJ = 512
TM_MERGE = 512
TM_FINAL = 512
TK_ROUTE = 256
TM_FFN = 1024
EB_FFN = 1024
ATTN_BLK = 128
ATTN_STEP = 512
SATTN_SEQS = 4


def _unpack(packed):
    return pltpu.bitcast(packed, BF16)


def _pack(x):
    return pltpu.bitcast(x.astype(BF16), U32)


def _pack_kernel(w_ref, o_ref, *, transpose):
    w = w_ref[...]
    if transpose:
        w = w.T
    o_ref[...] = pltpu.bitcast(w.astype(BF16), U32)


def _pack_weight(w, transpose=False):
    r, c = w.shape
    tr = next(t for t in (1024, 512, 256, 128) if r % t == 0)
    tc = next(t for t in (1024, 512, 256, 128) if c % t == 0)
    if transpose:
        out_shape, out_spec = (c // 2, r), pl.BlockSpec((tc // 2, tr), lambda i, j: (j, i))
    else:
        out_shape, out_spec = (r // 2, c), pl.BlockSpec((tr // 2, tc), lambda i, j: (i, j))
    return pl.pallas_call(
        functools.partial(_pack_kernel, transpose=transpose),
        out_shape=jax.ShapeDtypeStruct(out_shape, U32),
        grid=(r // tr, c // tc),
        in_specs=[pl.BlockSpec((tr, tc), lambda i, j: (i, j))],
        out_specs=out_spec,
        compiler_params=_cparams(("arbitrary", "arbitrary")),
        name="pack_weight",
    )(w)


def _cparams(sem, flags=None):
    return pltpu.CompilerParams(dimension_semantics=sem, vmem_limit_bytes=VMEM_LIMIT, flags=flags)


def _ada_kernel(c_ref, w_ref, b_ref, o_ref):
    c = c_ref[...]
    a = (c * jax.nn.sigmoid(c)).astype(BF16)
    o_ref[...] = jnp.dot(a, w_ref[...].astype(BF16), preferred_element_type=F32) + b_ref[...]


def _ada(c_all, w_ada, b_ada):
    rows = c_all.shape[0]
    n_out = w_ada.shape[1]
    tn = 1024
    return pl.pallas_call(
        _ada_kernel,
        out_shape=jax.ShapeDtypeStruct((rows, n_out), F32),
        grid=(n_out // tn,),
        in_specs=[pl.BlockSpec((rows, D_MODEL), lambda j: (0, 0)),
                  pl.BlockSpec((D_MODEL, tn), lambda j: (0, j)),
                  pl.BlockSpec((1, tn), lambda j: (0, j))],
        out_specs=pl.BlockSpec((rows, tn), lambda j: (0, j)),
        compiler_params=_cparams(("arbitrary",)),
        name="adaln",
    )(c_all, w_ada, b_ada.reshape(1, n_out))


def _rmsnorm_mod(x, g, scale, shift):
    ms = jnp.mean(x * x, axis=-1, keepdims=True)
    y = x * lax.rsqrt(ms + NORM_EPS) * g
    return y * (1.0 + scale) + shift


def _inproj_kernel(x_ref, sc_ref, sh_ref, n1_ref, cos_ref, sa_ref, sb_ref, w_ref, *refs, by_residue):
    n_qkv = 3 * N_GROUPS if by_residue else 3
    bg_ref, u_ref, ga_ref, gc_ref, zbuf = refs[n_qkv:]
    n_chunks, tm = zbuf.shape[0], zbuf.shape[1]
    per_group = GROUP_WIDTH // LANES
    h = _rmsnorm_mod(x_ref[...], n1_ref[...], sc_ref[...], sh_ref[...]).astype(BF16)

    def proj(lo, hi):
        return jnp.dot(h, _unpack(w_ref[:, lo:hi]), preferred_element_type=F32)

    cos, sa, sb = cos_ref[...], sa_ref[...], sb_ref[...]

    def stage(z, rope, mult):
        for c in range(n_chunks):
            zc = z[:, c * LANES:(c + 1) * LANES]
            if rope:
                zc = zc * cos + pltpu.roll(zc, LANES - 32, 1) * sa + pltpu.roll(zc, 32, 1) * sb
            zbuf[c] = zc * mult if mult != 1.0 else zc

    def emit(which):
        for c in range(n_chunks):
            if not by_residue:
                refs[which][:, c * LANES:(c + 1) * LANES] = zbuf[c].astype(refs[which].dtype)
                continue
            g, cl = c // per_group, c % per_group
            ref, d = refs[which * N_GROUPS + g], DILATIONS[g]
            for r in range(d):
                ref[r, :, cl * LANES:(cl + 1) * LANES] = zbuf[c, pl.ds(r, tm // d, stride=d), :].astype(ref.dtype)

    o = 0
    stage(proj(o, o + ATTN_WIDTH), True, ATTN_SCALE)
    emit(0)
    o += ATTN_WIDTH
    stage(proj(o, o + ATTN_WIDTH), True, 1.0)
    emit(1)
    o += ATTN_WIDTH
    stage(proj(o, o + ATTN_WIDTH), False, 1.0)
    emit(2)
    o += ATTN_WIDTH
    bg_ref[...] = _pack(proj(o, o + CONV_WIDTH))
    o += CONV_WIDTH
    cg = proj(o, o + CONV_WIDTH)
    o += CONV_WIDTH
    u_ref[...] = cg * proj(o, o + CONV_WIDTH)
    o += CONV_WIDTH
    ga_ref[...] = _pack(proj(o, o + D_MODEL))
    o += D_MODEL
    gc_ref[...] = _pack(proj(o, o + D_MODEL))


def _inproj(x2d, mod_specs, mods, tab_spec, tabs, n1, w_in_p, residue_seq=None):
    n = x2d.shape[0]
    tm = TM_PROJ
    row = lambda w: pl.BlockSpec((tm, w), lambda i: (i, 0))
    const = lambda s: pl.BlockSpec(s, lambda i: (0, 0), pipeline_mode=pl.Buffered(1))
    if residue_seq is None:
        qkv_shapes = [jax.ShapeDtypeStruct((n, ATTN_WIDTH), dt) for dt in (BF16, F32, F32)]
        qkv_specs = [row(ATTN_WIDTH)] * 3
    else:
        bps = residue_seq // tm
        qkv_shapes, qkv_specs = [], []
        for dt in (BF16, F32, F32):
            for d in DILATIONS:
                qkv_shapes.append(jax.ShapeDtypeStruct((n // residue_seq, d, residue_seq // d, GROUP_WIDTH), dt))
                qkv_specs.append(pl.BlockSpec((None, d, tm // d, GROUP_WIDTH), lambda i: (i // bps, 0, i % bps, 0)))
    half = lambda w: pl.BlockSpec((tm // 2, w), lambda i: (i, 0))
    outs = qkv_shapes + [jax.ShapeDtypeStruct((n // 2, CONV_WIDTH), U32),
                         jax.ShapeDtypeStruct((n, CONV_WIDTH), F32),
                         jax.ShapeDtypeStruct((n // 2, D_MODEL), U32),
                         jax.ShapeDtypeStruct((n // 2, D_MODEL), U32)]
    return pl.pallas_call(
        functools.partial(_inproj_kernel, by_residue=residue_seq is not None),
        out_shape=outs,
        grid=(n // tm,),
        in_specs=[row(D_MODEL), mod_specs, mod_specs, const((1, D_MODEL)),
                  tab_spec, tab_spec, tab_spec, const(w_in_p.shape)],
        out_specs=qkv_specs + [half(CONV_WIDTH), row(CONV_WIDTH), half(D_MODEL), half(D_MODEL)],
        scratch_shapes=[pltpu.VMEM((ATTN_WIDTH // LANES, tm, LANES), F32)],
        compiler_params=_cparams(("arbitrary",)),
        name="inproj",
    )(x2d, mods[0], mods[1], n1, tabs[0], tabs[1], tabs[2], w_in_p)


def _pattn_kernel(q_ref, kp_ref, kc_ref, vp_ref, vc_ref, o_ref, l_ref):
    nb = pl.program_id(2)
    k = jnp.concatenate([kp_ref[...], kc_ref[...]], axis=0).astype(BF16)
    v = jnp.concatenate([vp_ref[...], vc_ref[...]], axis=0).astype(BF16)
    qi = lax.broadcasted_iota(jnp.int32, (ATTN_BLK, 2 * ATTN_BLK), 0)
    ki = lax.broadcasted_iota(jnp.int32, (ATTN_BLK, 2 * ATTN_BLK), 1)
    dist = qi + ATTN_BLK - ki
    band = (dist >= 0) & (dist <= N_BACK)
    band0 = band & (ki >= jnp.where(nb > 0, 0, ATTN_BLK))
    for sb in range(q_ref.shape[0] // ATTN_BLK):
        rows = slice(sb * ATTN_BLK, (sb + 1) * ATTN_BLK)
        keys = slice(sb * ATTN_BLK, (sb + 2) * ATTN_BLK)
        q = q_ref[rows, :]
        valid = band0 if sb == 0 else band
        for h in range(HEADS_PER_GROUP):
            sl = slice(h * HEAD_DIM, (h + 1) * HEAD_DIM)
            s = lax.dot_general(q[:, sl], k[keys, sl], (((1,), (1,)), ((), ())), preferred_element_type=F32)
            s = jnp.where(valid, s, -jnp.inf)
            m = jnp.max(s, axis=-1, keepdims=True)
            p = jnp.exp(s - m)
            den = jnp.sum(p, axis=-1, keepdims=True)
            o = jnp.dot(p.astype(BF16), v[keys, sl], preferred_element_type=F32) / den
            o_ref[rows, sl] = o
            l_ref[rows, sl] = jnp.broadcast_to(m + jnp.log(den), (ATTN_BLK, HEAD_DIM))


def _pattn(q, k, v, g):
    batch, d, n, _ = q.shape
    step = min(ATTN_STEP, n)
    per_blk = step // ATTN_BLK
    cur = pl.BlockSpec((None, None, step, GROUP_WIDTH), lambda b, r, nb: (b, r, nb, 0))
    prev = pl.BlockSpec((None, None, ATTN_BLK, GROUP_WIDTH),
                        lambda b, r, nb: (b, r, jnp.maximum(nb * per_blk - 1, 0), 0))
    return pl.pallas_call(
        _pattn_kernel,
        out_shape=[jax.ShapeDtypeStruct((batch, d, n, GROUP_WIDTH), F32)] * 2,
        grid=(batch, d, n // step),
        in_specs=[cur, prev, cur, prev, cur],
        out_specs=[cur, cur],
        compiler_params=_cparams(("arbitrary",) * 3),
        name=f"pattn{g}",
    )(q, k, k, v, v)


def _sattn_kernel(q_ref, k_ref, v_ref, c0_ref, c1_ref, c2_ref, o_ref, *, s_len):
    for s in range(q_ref.shape[0]):
        _sattn_one(q_ref.at[s], k_ref.at[s], v_ref.at[s], c0_ref.at[s], c1_ref.at[s], c2_ref.at[s], o_ref.at[s],
                   s_len=s_len)


def _sattn_one(q_ref, k_ref, v_ref, c0_ref, c1_ref, c2_ref, o_ref, *, s_len):
    q = q_ref[...].astype(F32)
    kn = k_ref[...]
    vn = v_ref[...]
    rows = HEADS_PER_GROUP * s_len
    lane_head = lax.broadcasted_iota(jnp.int32, (s_len, GROUP_WIDTH), 1) // HEAD_DIM
    pad = jnp.zeros((LANES - s_len, GROUP_WIDTH), F32)
    nt_dims = (((1,), (1,)), ((), ()))
    ms, ls, os_ = [], [], []
    for g, cref in enumerate((c0_ref, c1_ref, c2_ref)):
        d, win = DILATIONS[g], WINDOWS[g]
        gs = slice(g * GROUP_WIDTH, (g + 1) * GROUP_WIDTH)
        qg = q[:, gs]
        qexp = jnp.concatenate(
            [jnp.where(lane_head == h, qg, 0.0) for h in range(HEADS_PER_GROUP)], axis=0).astype(BF16)
        s_c = jnp.dot(qexp, cref[0].astype(BF16), preferred_element_type=F32)
        s_n = lax.dot_general(qexp, jnp.concatenate([kn[:, gs], pad], axis=0).astype(BF16), nt_dims,
                              preferred_element_type=F32)
        back_c = (win + lax.broadcasted_iota(jnp.int32, (rows, win), 0) % s_len
                  - lax.broadcasted_iota(jnp.int32, (rows, win), 1))
        col_n = lax.broadcasted_iota(jnp.int32, (rows, LANES), 1)
        back_n = lax.broadcasted_iota(jnp.int32, (rows, LANES), 0) % s_len - col_n
        s_c = jnp.where((back_c <= win) & ((back_c & (d - 1)) == 0), s_c, -jnp.inf)
        s_n = jnp.where((back_n >= 0) & ((back_n & (d - 1)) == 0) & (col_n < s_len), s_n, -jnp.inf)
        m = jnp.maximum(jnp.max(s_c, axis=-1, keepdims=True), jnp.max(s_n, axis=-1, keepdims=True))
        p_c = jnp.exp(s_c - m)
        p_n = jnp.exp(s_n - m)
        ls.append(jnp.sum(p_c, axis=-1, keepdims=True) + jnp.sum(p_n, axis=-1, keepdims=True))
        ms.append(m)
        o = lax.dot_general(p_c.astype(BF16), cref[1].astype(BF16), nt_dims, preferred_element_type=F32)
        o = o + jnp.dot(p_n.astype(BF16), jnp.concatenate([vn[:, gs], pad], axis=0).astype(BF16),
                        preferred_element_type=F32)
        os_.append(o)
    mm = jnp.maximum(jnp.maximum(ms[0], ms[1]), ms[2])
    num = jnp.zeros((rows, GROUP_WIDTH), F32)
    den = jnp.zeros((rows, 1), F32)
    for g in range(N_GROUPS):
        w = jnp.exp(ms[g] - mm)
        num = num + w * os_[g]
        den = den + w * ls[g]
    full = num / den
    out = jnp.zeros((s_len, GROUP_WIDTH), F32)
    for h in range(HEADS_PER_GROUP):
        out = out + jnp.where(lane_head == h, full[h * s_len:(h + 1) * s_len], 0.0)
    o_ref[...] = out


def _sattn(q, k, v, caches, n_seq, s_len):
    q3 = q.reshape(n_seq, s_len, ATTN_WIDTH)
    k3 = k.reshape(n_seq, s_len, ATTN_WIDTH)
    v3 = v.reshape(n_seq, s_len, ATTN_WIDTH)
    cts = [jnp.transpose(c, (0, 2, 3, 4, 1)).reshape(n_seq, 2, GROUP_WIDTH, c.shape[1]) for c in caches]
    ns = SATTN_SEQS
    tok = pl.BlockSpec((ns, s_len, ATTN_WIDTH), lambda b: (b, 0, 0))
    out = pl.pallas_call(
        functools.partial(_sattn_kernel, s_len=s_len),
        out_shape=jax.ShapeDtypeStruct((n_seq, s_len, GROUP_WIDTH), F32),
        grid=(n_seq // ns,),
        in_specs=[tok, tok, tok] + [pl.BlockSpec((ns, 2, GROUP_WIDTH, w), lambda b: (b, 0, 0, 0)) for w in WINDOWS],
        out_specs=pl.BlockSpec((ns, s_len, GROUP_WIDTH), lambda b: (b, 0, 0)),
        compiler_params=_cparams(("arbitrary",)),
        name="sattn",
    )(q3, k3, v3, *cts)
    return out.reshape(n_seq * s_len, GROUP_WIDTH)


def _merge_tail(o_attn, bg, yc, ga_ref, gc_ref, x_ref, g1_ref, n2_ref, sc2_ref, sh2_ref,
                wa_ref, wc_ref, wo_ref, x1_ref, h2t_ref):
    a_out = jnp.dot(o_attn.astype(BF16), _unpack(wa_ref[...]), preferred_element_type=F32)
    c_out = jnp.dot((bg * yc).astype(BF16), _unpack(wc_ref[...]), preferred_element_type=F32)
    ga, gc = _unpack(ga_ref[...]).astype(F32), _unpack(gc_ref[...]).astype(F32)
    mix = jax.nn.sigmoid(ga) * a_out + jax.nn.sigmoid(gc) * c_out
    mo = jnp.dot(mix.astype(BF16), _unpack(wo_ref[...]), preferred_element_type=F32)
    x1 = x_ref[...] + g1_ref[...] * mo
    x1_ref[...] = x1
    h2 = _rmsnorm_mod(x1, n2_ref[...], sc2_ref[...], sh2_ref[...])
    h2t_ref[...] = pltpu.bitcast(h2.T.astype(BF16), U32)


def _conv3(u, um1, um2, cw_ref):
    return cw_ref[0:1, :] * um2 + cw_ref[1:2, :] * um1 + cw_ref[2:3, :] * u


def _merge_prompt_kernel(o0_ref, o1_ref, o2_ref, l0_ref, l1_ref, l2_ref, bg_ref, u_ref, uh_ref, cw_ref,
                         ga_ref, gc_ref, x_ref, g1_ref, n2_ref, sc2_ref, sh2_ref, wa_ref, wc_ref, wo_ref,
                         x1_ref, h2t_ref, ubuf, *rowbufs, blocks_per_seq):
    tm = u_ref.shape[0]
    first = (pl.program_id(0) % blocks_per_seq) == 0

    def by_position(ref, buf):
        d, rows = ref.shape[0], ref.shape[1]
        if d == 1:
            return ref[0]
        for r in range(d):
            for c in range(GROUP_WIDTH // LANES):
                buf[c, pl.ds(r, rows, stride=d), :] = ref[r, :, c * LANES:(c + 1) * LANES]
        return jnp.concatenate([buf[c] for c in range(GROUP_WIDTH // LANES)], axis=1)

    o0, l0 = by_position(o0_ref, None), by_position(l0_ref, None)
    o1, l1 = by_position(o1_ref, rowbufs[0]), by_position(l1_ref, rowbufs[1])
    o2, l2 = by_position(o2_ref, rowbufs[2]), by_position(l2_ref, rowbufs[3])
    mm = jnp.maximum(jnp.maximum(l0, l1), l2)
    e0, e1, e2 = jnp.exp(l0 - mm), jnp.exp(l1 - mm), jnp.exp(l2 - mm)
    o_attn = (e0 * o0 + e1 * o1 + e2 * o2) / (e0 + e1 + e2)
    u = u_ref[...]
    ubuf[0:SUBLANES, :] = jnp.where(first, 0.0, uh_ref[...])
    ubuf[SUBLANES:SUBLANES + tm, :] = u
    yc = _conv3(u, ubuf[SUBLANES - 1:SUBLANES - 1 + tm, :], ubuf[SUBLANES - 2:SUBLANES - 2 + tm, :], cw_ref)
    _merge_tail(o_attn, _unpack(bg_ref[...]).astype(F32), yc, ga_ref, gc_ref, x_ref, g1_ref, n2_ref, sc2_ref, sh2_ref,
                wa_ref, wc_ref, wo_ref, x1_ref, h2t_ref)


def _merge_sample_kernel(oa_ref, bg_ref, u_ref, p1_ref, p2_ref, cw_ref,
                         ga_ref, gc_ref, x_ref, g1_ref, n2_ref, sc2_ref, sh2_ref, wa_ref, wc_ref, wo_ref,
                         h2t_all_ref, x1_ref, h2t_ref, ubuf, *, s_len):
    del h2t_all_ref
    tm = u_ref.shape[0]
    u = u_ref[...]
    ubuf[0:SUBLANES, :] = jnp.zeros((SUBLANES, CONV_WIDTH), F32)
    ubuf[SUBLANES:SUBLANES + tm, :] = u
    t = lax.broadcasted_iota(jnp.int32, (tm, CONV_WIDTH), 0) % s_len
    um1 = jnp.where(t < 1, p1_ref[...], ubuf[SUBLANES - 1:SUBLANES - 1 + tm, :])
    um2 = jnp.where(t < 2, p2_ref[...], ubuf[SUBLANES - 2:SUBLANES - 2 + tm, :])
    yc = _conv3(u, um1, um2, cw_ref)
    _merge_tail(oa_ref[...], _unpack(bg_ref[...]).astype(F32), yc, ga_ref, gc_ref, x_ref, g1_ref, n2_ref, sc2_ref, sh2_ref,
                wa_ref, wc_ref, wo_ref, x1_ref, h2t_ref)


def _merge_common_specs(tm, mod_spec):
    row = lambda w: pl.BlockSpec((tm, w), lambda i: (i, 0))
    const = lambda s: pl.BlockSpec(s, lambda i: (0, 0), pipeline_mode=pl.Buffered(1))
    half = lambda w: pl.BlockSpec((tm // 2, w), lambda i: (i, 0))
    ins = [half(D_MODEL), half(D_MODEL), row(D_MODEL), mod_spec, const((1, D_MODEL)), mod_spec, mod_spec,
           const((GROUP_WIDTH // 2, D_MODEL)), const((CONV_WIDTH // 2, D_MODEL)), const((D_MODEL // 2, D_MODEL))]
    outs = [row(D_MODEL), pl.BlockSpec((D_MODEL // 2, tm), lambda i: (0, i))]
    return ins, outs


def _merge_prompt(attn, bg, u, conv_w, ga, gc, x2d, mod_spec, g1, n2, sc2, sh2, wa, wc, wo, seq, n_total):
    n = x2d.shape[0]
    tm = TM_MERGE
    row = lambda w: pl.BlockSpec((tm, w), lambda i: (i, 0))
    halo = pl.BlockSpec((SUBLANES, CONV_WIDTH), lambda i: (jnp.maximum(i * (tm // SUBLANES) - 1, 0), 0))
    ins, outs = _merge_common_specs(tm, mod_spec)
    (o0, l0), (o1, l1), (o2, l2) = attn
    bps = seq // tm
    grp = [pl.BlockSpec((None, d, tm // d, GROUP_WIDTH), lambda i: (i // bps, 0, i % bps, 0)) for d in DILATIONS]
    return pl.pallas_call(
        functools.partial(_merge_prompt_kernel, blocks_per_seq=bps),
        out_shape=[jax.ShapeDtypeStruct((n, D_MODEL), F32), jax.ShapeDtypeStruct((D_MODEL // 2, n_total), U32)],
        grid=(n // tm,),
        in_specs=grp + grp + [pl.BlockSpec((tm // 2, CONV_WIDTH), lambda i: (i, 0)), row(CONV_WIDTH), halo,
                              pl.BlockSpec((CONV_K, CONV_WIDTH), lambda i: (0, 0))] + ins,
        out_specs=outs,
        scratch_shapes=[pltpu.VMEM((SUBLANES + tm, CONV_WIDTH), F32)] + [pltpu.VMEM((GROUP_WIDTH // LANES, tm, LANES), F32)] * 4,
        compiler_params=_cparams(("arbitrary",)),
        name="merge_prompt",
    )(o0, o1, o2, l0, l1, l2, bg, u, u, conv_w, ga, gc, x2d, g1, n2, sc2, sh2, wa, wc, wo)


def _merge_sample(o_attn, bg, u, p1, p2, conv_w, ga, gc, x2d, mod_spec, g1, n2, sc2, sh2, wa, wc, wo, s_len,
                  h2t_all, col_off):
    n = x2d.shape[0]
    tm = TM_MERGE
    row = lambda w: pl.BlockSpec((tm, w), lambda i: (i, 0))
    ins, outs = _merge_common_specs(tm, mod_spec)
    outs = [outs[0], pl.BlockSpec((D_MODEL // 2, tm), lambda i: (0, i + col_off // tm))]
    in_specs = ([row(GROUP_WIDTH), pl.BlockSpec((tm // 2, CONV_WIDTH), lambda i: (i, 0))] + [row(CONV_WIDTH)] * 3
                + [pl.BlockSpec((CONV_K, CONV_WIDTH), lambda i: (0, 0))] + ins + [pl.BlockSpec(memory_space=pl.ANY)])
    return pl.pallas_call(
        functools.partial(_merge_sample_kernel, s_len=s_len),
        out_shape=[jax.ShapeDtypeStruct((n, D_MODEL), F32), jax.ShapeDtypeStruct(h2t_all.shape, U32)],
        grid=(n // tm,),
        in_specs=in_specs,
        out_specs=outs,
        input_output_aliases={len(in_specs) - 1: 1},
        scratch_shapes=[pltpu.VMEM((SUBLANES + tm, CONV_WIDTH), F32)],
        compiler_params=_cparams(("arbitrary",)),
        name="merge_sample",
    )(o_attn, bg, u, p1, p2, conv_w, ga, gc, x2d, g1, n2, sc2, sh2, wa, wc, wo, h2t_all)


def _oddeven_merge_sort_pairs(n):
    pairs = []
    p = 1
    while p < n:
        k = p
        while k >= 1:
            for j in range(k % p, n - k, 2 * k):
                for i in range(min(k, n - j - k)):
                    if (i + j) // (2 * p) == (i + j + k) // (2 * p):
                        pairs.append((i + j, i + j + k))
            k //= 2
        p *= 2
    return pairs


_SORT16 = _oddeven_merge_sort_pairs(PEER_TOPK)


def _cmpx(x, a, b):
    hi, lo = jnp.maximum(x[a], x[b]), jnp.minimum(x[a], x[b])
    x[a], x[b] = hi, lo


def _bitonic_clean(x):
    stride = PEER_TOPK // 2
    while stride >= 1:
        for i in range(PEER_TOPK):
            if i & stride == 0:
                _cmpx(x, i, i + stride)
        stride //= 2
    return x


def _merge_top16(a, b):
    c = []
    for j in range(PEER_TOPK):
        jb = PEER_TOPK - 1 - j
        c.append(jnp.maximum(a[j], b[jb]) if jb < len(b) else a[j])
    return _bitonic_clean(c)


def _top16_over_keys(s):
    x = [s[j * SUBLANES:(j + 1) * SUBLANES, :] for j in range(N_KEYS // SUBLANES)]
    for a, b in _SORT16:
        _cmpx(x, a, b)
    shift = SUBLANES // 2
    while shift >= 1:
        x = _merge_top16(x, [pltpu.roll(v, shift, 0) for v in x])
        shift //= 2
    return x


def _route_kernel(h_ref, wq_ref, k1_ref, k2_ref, cnt_ref, m1_ref, r2_ref, e2_ref):
    tk = h_ref.shape[1]
    qt = jnp.dot(_unpack(wq_ref[...]), _unpack(h_ref[...]), preferred_element_type=F32).astype(BF16)
    k1, k2 = k1_ref[...], k2_ref[...]
    sub = lax.broadcasted_iota(jnp.int32, (SUBLANES, tk), 0)
    half = N_KEYS
    s1s, s2s = [], []
    v1 = [jnp.zeros((SUBLANES, tk), F32) for _ in range(PEER_TOPK)]
    v2 = [jnp.zeros((SUBLANES, tk), F32) for _ in range(PEER_TOPK)]
    for h in range(PEER_HEADS):
        base = h * 2 * half
        s1 = jnp.dot(k1, qt[base:base + half], preferred_element_type=F32)
        s2 = jnp.dot(k2, qt[base + half:base + 2 * half], preferred_element_type=F32)
        s1s.append(s1)
        s2s.append(s2)
        t1 = _top16_over_keys(s1)
        t2 = _top16_over_keys(s2)
        for j in range(PEER_TOPK):
            v1[j] = jnp.where(sub == h, t1[j], v1[j])
            v2[j] = jnp.where(sub == h, t2[j], v2[j])
    psum = {}
    for a in range(PEER_TOPK):
        for b in range(PEER_TOPK // (a + 1)):
            psum[a, b] = v1[a] + v2[b]
    lists = [[psum[a, b] for b in range(PEER_TOPK // (a + 1))] for a in range(PEER_TOPK // 2)]
    lists.append([psum[a, 0] for a in range(PEER_TOPK // 2, PEER_TOPK)])
    top = lists[0]
    for other in lists[1:]:
        top = _merge_top16(top, other)
    tau = top[PEER_TOPK - 1]
    z = jnp.ones((SUBLANES, tk), F32)
    for j in range(1, PEER_TOPK):
        z = z + jnp.exp(top[j] - top[0])
    rz = 1.0 / z
    x1 = []
    for b in range(PEER_TOPK):
        x = jnp.full((SUBLANES, tk), jnp.inf, F32)
        for a in range(PEER_TOPK // (b + 1)):
            x = jnp.where(psum[a, b] >= tau, v1[a], x)
        x1.append(x)
    for h in range(PEER_HEADS):
        s1, s2 = s1s[h], s2s[h]
        row = lambda v: v[h:h + 1, :]
        cnt = jnp.zeros_like(s1)
        rank = jnp.full_like(s2, float(PEER_TOPK))
        for j in range(PEER_TOPK):
            cnt = jnp.where(s1 >= row(x1[j]), float(j + 1), cnt)
        for j in range(PEER_TOPK - 1, -1, -1):
            rank = jnp.where(s2 >= row(v2[j]), float(j), rank)
        m1 = jnp.exp(s1 - row(v1[0])) * row(rz)
        cnt_ref[:, h, :, :] = cnt.reshape(N_KEYS // SUBLANES, SUBLANES, tk)
        m1_ref[:, h, :, :] = m1.reshape(N_KEYS // SUBLANES, SUBLANES, tk)
        half_rows = slice(h * N_KEYS // 2, (h + 1) * N_KEYS // 2)
        r2_ref[half_rows, :] = pltpu.bitcast(rank.astype(BF16), U32)
        e2_ref[half_rows, :] = pltpu.bitcast(jnp.exp(s2 - row(v2[0])).astype(BF16), U32)


def _route(h2t, wq_t, k1, k2):
    n = h2t.shape[1]
    tk = TK_ROUTE
    a8 = N_KEYS // SUBLANES
    s14 = jax.ShapeDtypeStruct((a8, PEER_HEADS, SUBLANES, n), F32)
    s2d = jax.ShapeDtypeStruct((PEER_HEADS * N_KEYS // 2, n), U32)
    spec4 = pl.BlockSpec((a8, PEER_HEADS, SUBLANES, tk), lambda i: (0, 0, 0, i))
    spec2 = pl.BlockSpec((PEER_HEADS * N_KEYS // 2, tk), lambda i: (0, i))
    return pl.pallas_call(
        _route_kernel,
        out_shape=[s14, s14, s2d, s2d],
        grid=(n // tk,),
        in_specs=[pl.BlockSpec((D_MODEL // 2, tk), lambda i: (0, i)),
                  pl.BlockSpec(wq_t.shape, lambda i: (0, 0)),
                  pl.BlockSpec(k1.shape, lambda i: (0, 0)),
                  pl.BlockSpec(k2.shape, lambda i: (0, 0))],
        out_specs=[spec4, spec4, spec2, spec2],
        compiler_params=_cparams(("arbitrary",)),
        name="peer_route",
    )(h2t, wq_t, k1, k2)


FFN_MXU_TILE = 256
FFN_VPU_TILE = LANES
BF16_ROWS = 16
FFN_ROW_CHUNK = 64


def _ffn_kernel(h_ref, u_ref, vt_ref, cnt_ref, m1_ref, r2_ref, e2_ref, o_ref, coef_ref, *act_refs):
    tm = h_ref.shape[1]
    n_bt = N_KEYS // BF16_ROWS

    def row16(ref, h, al, ls):
        row = ref[al // SUBLANES, h, al % SUBLANES:al % SUBLANES + 1, ls]
        return jnp.broadcast_to(row, (BF16_ROWS, FFN_VPU_TILE)).astype(BF16)

    def mxu_cols(mt):
        return slice(mt * FFN_MXU_TILE, (mt + 1) * FFN_MXU_TILE)

    def build_gates(mt):
        for vt in range(FFN_MXU_TILE // FFN_VPU_TILE):
            lo = mt * FFN_MXU_TILE + vt * FFN_VPU_TILE
            ls = slice(lo, lo + FFN_VPU_TILE)
            for a0 in range(0, EB_FFN // N_KEYS, 2):
                acc = [[jnp.zeros((BF16_ROWS, FFN_VPU_TILE), BF16) for _ in range(n_bt)] for _ in range(2)]
                for h in range(PEER_HEADS):
                    cnts = [row16(cnt_ref, h, a0 + k, ls) for k in range(2)]
                    m1s = [row16(m1_ref, h, a0 + k, ls) for k in range(2)]
                    for bt in range(n_bt):
                        rs = slice((h * N_KEYS + bt * BF16_ROWS) // 2, (h * N_KEYS + (bt + 1) * BF16_ROWS) // 2)
                        rank, e2 = _unpack(r2_ref[rs, ls]), _unpack(e2_ref[rs, ls])
                        for k in range(2):
                            acc[k][bt] = acc[k][bt] + jnp.where(rank < cnts[k], e2, jnp.zeros_like(e2)) * m1s[k]
                for k in range(2):
                    for bt in range(n_bt):
                        es = slice((a0 + k) * N_KEYS + bt * BF16_ROWS, (a0 + k) * N_KEYS + (bt + 1) * BF16_ROWS)
                        coef_ref[es, ls] = acc[k][bt]

    def up_proj(mt):
        act_refs[mt % 2][...] = jnp.dot(_unpack(u_ref[...]), _unpack(h_ref[:, mxu_cols(mt)]),
                                        preferred_element_type=F32)

    def down_proj(mt):
        ms = mxu_cols(mt)
        k0 = (2.0 / jnp.pi) ** 0.5
        zero = jnp.minimum(pl.program_id(1), 0)
        for rc in range(EB_FFN // FFN_ROW_CHUNK):
            rs = slice(rc * FFN_ROW_CHUNK, (rc + 1) * FFN_ROW_CHUNK)
            x = act_refs[mt % 2][pl.ds(pl.multiple_of(zero + rc * FFN_ROW_CHUNK, FFN_ROW_CHUNK), FFN_ROW_CHUNK), :]
            c0 = -2.0 * k0 * LOG2_E
            gelu = x / (1.0 + jnp.exp2(x * (c0 + (c0 * 0.044715) * (x * x))))
            coef_ref[rs, ms] = coef_ref[rs, ms] * gelu.astype(BF16)
        o_ref[:, ms] += jnp.dot(_unpack(vt_ref[...]), coef_ref[:, ms], preferred_element_type=F32)

    @pl.when(pl.program_id(1) == 0)
    def _():
        o_ref[...] = jnp.zeros_like(o_ref)

    n_mt = tm // FFN_MXU_TILE
    build_gates(0)
    up_proj(0)
    for mt in range(n_mt):
        if mt + 1 < n_mt:
            build_gates(mt + 1)
            up_proj(mt + 1)
        down_proj(mt)


def _ffn(h2t, u_b, vt_b, cnt, m1, r2, e2):
    n = h2t.shape[1]
    tm, eb = TM_FFN, EB_FFN
    n_exp = 2 * u_b.shape[0]
    spec4 = pl.BlockSpec((eb // (N_KEYS * SUBLANES), PEER_HEADS, SUBLANES, tm), lambda i, e: (e, 0, 0, i))
    spec2 = pl.BlockSpec((PEER_HEADS * N_KEYS // 2, tm), lambda i, e: (0, i))
    return pl.pallas_call(
        _ffn_kernel,
        out_shape=jax.ShapeDtypeStruct((D_MODEL, n), F32),
        grid=(n // tm, n_exp // eb),
        in_specs=[pl.BlockSpec((D_MODEL // 2, tm), lambda i, e: (0, i)),
                  pl.BlockSpec((eb // 2, D_MODEL), lambda i, e: (e, 0)),
                  pl.BlockSpec((D_MODEL // 2, eb), lambda i, e: (0, e)),
                  spec4, spec4, spec2, spec2],
        out_specs=pl.BlockSpec((D_MODEL, tm), lambda i, e: (0, i)),
        scratch_shapes=[pltpu.VMEM((eb, tm), BF16)] + [pltpu.VMEM((eb, FFN_MXU_TILE), F32)] * 2,
        compiler_params=_cparams(("arbitrary", "arbitrary")),
        name="peer_ffn",
    )(h2t, u_b, vt_b, cnt, m1, r2, e2)


def _final_kernel(p_ref, x_ref, g2_ref, nf_ref, y_ref):
    x2 = x_ref[...] + g2_ref[...] * p_ref[...].T
    ms = jnp.mean(x2 * x2, axis=-1, keepdims=True)
    y_ref[...] = x2 * lax.rsqrt(ms + NORM_EPS) * nf_ref[...]


def _final(peer_t, col_off, x1, mod_spec, g2, nf):
    n = x1.shape[0]
    tm = TM_FINAL
    off = col_off // tm
    return pl.pallas_call(
        _final_kernel,
        out_shape=jax.ShapeDtypeStruct((n, D_MODEL), F32),
        grid=(n // tm,),
        in_specs=[pl.BlockSpec((D_MODEL, tm), lambda i: (0, i + off)),
                  pl.BlockSpec((tm, D_MODEL), lambda i: (i, 0)),
                  mod_spec,
                  pl.BlockSpec((1, D_MODEL), lambda i: (0, 0))],
        out_specs=pl.BlockSpec((tm, D_MODEL), lambda i: (i, 0)),
        compiler_params=_cparams(("arbitrary",)),
        name="final",
    )(peer_t, x1, g2, nf)


def _rope_tables(pos):
    half = HEAD_DIM // 2
    inv = ROPE_THETA ** (-jnp.arange(half, dtype=F32) / half)
    ang = pos.astype(F32)[:, None] * inv[None, :]
    cos, sin = jnp.cos(ang), jnp.sin(ang)
    zero = jnp.zeros_like(sin)
    reps = LANES // HEAD_DIM
    cos_t = jnp.tile(jnp.concatenate([cos, cos], axis=1), (1, reps))
    sa_t = jnp.tile(jnp.concatenate([-sin, zero], axis=1), (1, reps))
    sb_t = jnp.tile(jnp.concatenate([zero, sin], axis=1), (1, reps))
    return cos_t, sa_t, sb_t


def kernel(x_prompt, x_sample, cache_kv0, cache_kv1, cache_kv2, state_conv, c_prompt, c_sample,
           norm1_g, norm2_g, norm_f_g, w_ada, b_ada, w_in, conv_w, w_attn_o, w_conv_o, w_o,
           w_query, sub_keys, expert_u, expert_v):
    batch, seq, _ = x_prompt.shape
    n_seq, s_len, _ = x_sample.shape
    depth = w_in.shape[0]
    assert depth == 1
    assert tuple(c.shape[2] for c in (cache_kv0, cache_kv1, cache_kv2)) == WINDOWS
    assert s_len == SUBLANES and seq % (ATTN_BLK * DILATIONS[2]) == 0 and n_seq % SATTN_SEQS == 0
    n_p, n_s = batch * seq, n_seq * s_len
    l = 0

    w_in_b = _pack_weight(w_in[l])
    wa_b, wc_b, wo_b = _pack_weight(w_attn_o[l]), _pack_weight(w_conv_o[l]), _pack_weight(w_o[l])
    wq_t = _pack_weight(w_query[l], transpose=True)
    k1_b, k2_b = sub_keys[l, 0].astype(BF16), sub_keys[l, 1].astype(BF16)
    u_b = _pack_weight(expert_u[l])
    vt_b = _pack_weight(expert_v[l], transpose=True)
    n1, n2, nf = norm1_g[l][None, :], norm2_g[l][None, :], norm_f_g[None, :]

    n_c = batch + n_seq
    c_all = jnp.concatenate([c_prompt, c_sample, jnp.zeros((-n_c % SUBLANES, D_MODEL), F32)], axis=0)
    mod = _ada(c_all, w_ada[l], b_ada[l])
    mod_p = [m[:, None, :] for m in jnp.split(mod[:batch], 6, axis=-1)]
    mod_s = jnp.split(jnp.repeat(mod[batch:n_c], s_len, axis=0), 6, axis=-1)

    def seq_spec(tm):
        return pl.BlockSpec((None, 1, D_MODEL), lambda i: (i // (seq // tm), 0, 0))

    def tok_spec(tm):
        return pl.BlockSpec((tm, D_MODEL), lambda i: (i, 0))

    tabs_p = _rope_tables(jnp.arange(seq))
    tabs_s = [jnp.tile(t, (n_seq, 1)) for t in _rope_tables(PAST_LEN + jnp.arange(s_len))]
    tab_p_spec = pl.BlockSpec((TM_PROJ, LANES), lambda i: (i % (seq // TM_PROJ), 0))
    tab_s_spec = pl.BlockSpec((TM_PROJ, LANES), lambda i: (i, 0))

    xp = x_prompt.reshape(n_p, D_MODEL)
    xs = x_sample.reshape(n_s, D_MODEL)
    *qkv_p, bgp, up, gap, gcp = _inproj(xp, seq_spec(TM_PROJ), (mod_p[1], mod_p[0]), tab_p_spec, tabs_p, n1, w_in_b,
                                        residue_seq=seq)
    qp, kp, vp = qkv_p[0:N_GROUPS], qkv_p[N_GROUPS:2 * N_GROUPS], qkv_p[2 * N_GROUPS:]
    qs, ks, vs, bgs, us, gas, gcs = _inproj(xs, tok_spec(TM_PROJ), (mod_s[1], mod_s[0]), tab_s_spec, tabs_s, n1, w_in_b)

    attn_p = [_pattn(qp[g], kp[g], vp[g], g) for g in range(N_GROUPS)]
    caches = (cache_kv0[l], cache_kv1[l], cache_kv2[l])
    attn_s = _sattn(qs, ks, vs, caches, n_seq, s_len)

    x1p, h2tp = _merge_prompt(attn_p, bgp, up, conv_w[l], gap, gcp, xp, seq_spec(TM_MERGE),
                              mod_p[2], n2, mod_p[4], mod_p[3], wa_b, wc_b, wo_b, seq, n_p + n_s)
    st = state_conv[l]
    p1 = jnp.pad(st[:, 1:2], ((0, 0), (0, s_len - 1), (0, 0))).reshape(n_s, CONV_WIDTH)
    p2 = jnp.pad(st, ((0, 0), (0, s_len - 2), (0, 0))).reshape(n_s, CONV_WIDTH)
    x1s, h2t = _merge_sample(attn_s, bgs, us, p1, p2, conv_w[l], gas, gcs, xs, tok_spec(TM_MERGE),
                             mod_s[2], n2, mod_s[4], mod_s[3], wa_b, wc_b, wo_b, s_len, h2tp, n_p)
    cnt, m1, r2, e2 = _route(h2t, wq_t, k1_b, k2_b)
    peer_t = _ffn(h2t, u_b, vt_b, cnt, m1, r2, e2)

    yp = _final(peer_t, 0, x1p, seq_spec(TM_FINAL), mod_p[5], nf).reshape(batch, seq, D_MODEL)
    ys = _final(peer_t, n_p, x1s, tok_spec(TM_FINAL), mod_s[5], nf).reshape(n_seq, s_len, D_MODEL)

    def prompt_tail(a, g):
        d, n_rows = a.shape[1], a.shape[2]
        t = jnp.swapaxes(a[:, :, n_rows - N_BACK:, :], 1, 2)
        return t.reshape(batch, N_BACK * d, HEADS_PER_GROUP, HEAD_DIM)

    kv_p, kv_s = [], []
    for g in range(N_GROUPS):
        assert WINDOWS[g] == N_BACK * DILATIONS[g] <= seq
        gs = slice(g * GROUP_WIDTH, (g + 1) * GROUP_WIDTH)
        kv_p.append(jnp.stack([prompt_tail(kp[g], g), prompt_tail(vp[g], g)], axis=2)[None])
        k4 = ks.reshape(n_seq, s_len, ATTN_WIDTH)[:, :, gs].reshape(n_seq, s_len, HEADS_PER_GROUP, HEAD_DIM)
        v4 = vs.reshape(n_seq, s_len, ATTN_WIDTH)[:, :, gs].reshape(n_seq, s_len, HEADS_PER_GROUP, HEAD_DIM)
        kv_s.append(jnp.stack([k4, v4], axis=2)[None])
    conv_p = up.reshape(batch, seq, CONV_WIDTH)[:, seq - (CONV_K - 1):][None]
    conv_s = jnp.concatenate([st, us.reshape(n_seq, s_len, CONV_WIDTH)], axis=1)[:, s_len:][None]
    return (yp, ys, kv_p[0], kv_p[1], kv_p[2], conv_p, kv_s[0], kv_s[1], kv_s[2], conv_s)
```

```python
import functools

import jax
import jax.numpy as jnp
from jax import lax
from jax.experimental import pallas as pl
from jax.experimental.pallas import tpu as pltpu

F32 = jnp.float32
BF16 = jnp.bfloat16
U32 = jnp.uint32

D_MODEL = 1024
HEAD_DIM = 64
HEADS_PER_GROUP = 4
WINDOWS = (128, 512, 2048)
DILATIONS = (1, 4, 16)
N_GROUPS = 3
GROUP_WIDTH = HEADS_PER_GROUP * HEAD_DIM
ATTN_WIDTH = N_GROUPS * GROUP_WIDTH
CONV_WIDTH = 768
CONV_K = 3
IN_WIDTH = 3 * ATTN_WIDTH + 3 * CONV_WIDTH + 2 * D_MODEL
ATTN_SCALE = HEAD_DIM ** -0.5
ROPE_THETA = 10000.0
N_KEYS = 128
PEER_HEADS = 8
PEER_TOPK = 16
NORM_EPS = 1e-6
LOG2_E = 1.4426950408889634
N_BACK = 128
PAST_LEN = 8192

LANES = 128
SUBLANES = 8
VMEM_LIMIT = 52 * 1024 * 1024

TM_PROJ = 512
TM_MERGE = 512
TM_FINAL = 512
TK_ROUTE = 256
TM_FFN = 1024
EB_FFN = 1024
ATTN_BLK = 128
ATTN_STEP = 512
SATTN_SEQS = 4


def _unpack(packed):
    return pltpu.bitcast(packed, BF16)


def _pack(x):
    return pltpu.bitcast(x.astype(BF16), U32)


def _pack_kernel(w_ref, o_ref, *, transpose):
    w = w_ref[...]
    if transpose:
        w = w.T
    o_ref[...] = pltpu.bitcast(w.astype(BF16), U32)


def _pack_weight(w, transpose=False):
    r, c = w.shape
    tr = next(t for t in (1024, 512, 256, 128) if r % t == 0)
    tc = next(t for t in (1024, 512, 256, 128) if c % t == 0)
    if transpose:
        out_shape, out_spec = (c // 2, r), pl.BlockSpec((tc // 2, tr), lambda i, j: (j, i))
    else:
        out_shape, out_spec = (r // 2, c), pl.BlockSpec((tr // 2, tc), lambda i, j: (i, j))
    return pl.pallas_call(
        functools.partial(_pack_kernel, transpose=transpose),
        out_shape=jax.ShapeDtypeStruct(out_shape, U32),
        grid=(r // tr, c // tc),
        in_specs=[pl.BlockSpec((tr, tc), lambda i, j: (i, j))],
        out_specs=out_spec,
        compiler_params=_cparams(("arbitrary", "arbitrary")),
        name="pack_weight",
    )(w)


def _cparams(sem, flags=None):
    return pltpu.CompilerParams(dimension_semantics=sem, vmem_limit_bytes=VMEM_LIMIT, flags=flags)


def _ada_kernel(c_ref, w_ref, b_ref, o_ref):
    c = c_ref[...]
    a = (c * jax.nn.sigmoid(c)).astype(BF16)
    o_ref[...] = jnp.dot(a, w_ref[...].astype(BF16), preferred_element_type=F32) + b_ref[...]


def _ada(c_all, w_ada, b_ada):
    rows = c_all.shape[0]
    n_out = w_ada.shape[1]
    tn = 1024
    return pl.pallas_call(
        _ada_kernel,
        out_shape=jax.ShapeDtypeStruct((rows, n_out), F32),
        grid=(n_out // tn,),
        in_specs=[pl.BlockSpec((rows, D_MODEL), lambda j: (0, 0)),
                  pl.BlockSpec((D_MODEL, tn), lambda j: (0, j)),
                  pl.BlockSpec((1, tn), lambda j: (0, j))],
        out_specs=pl.BlockSpec((rows, tn), lambda j: (0, j)),
        compiler_params=_cparams(("arbitrary",)),
        name="adaln",
    )(c_all, w_ada, b_ada.reshape(1, n_out))


def _rmsnorm_mod(x, g, scale, shift):
    ms = jnp.mean(x * x, axis=-1, keepdims=True)
    y = x * lax.rsqrt(ms + NORM_EPS) * g
    return y * (1.0 + scale) + shift


def _inproj_kernel(x_ref, sc_ref, sh_ref, n1_ref, cos_ref, sa_ref, sb_ref, w_ref, *refs, by_residue):
    n_qkv = 3 * N_GROUPS if by_residue else 3
    bg_ref, u_ref, ga_ref, gc_ref, zbuf = refs[n_qkv:]
    n_chunks, tm = zbuf.shape[0], zbuf.shape[1]
    per_group = GROUP_WIDTH // LANES
    h = _rmsnorm_mod(x_ref[...], n1_ref[...], sc_ref[...], sh_ref[...]).astype(BF16)

    def proj(lo, hi):
        return jnp.dot(h, _unpack(w_ref[:, lo:hi]), preferred_element_type=F32)

    cos, sa, sb = cos_ref[...], sa_ref[...], sb_ref[...]

    def stage(z, rope, mult):
        for c in range(n_chunks):
            zc = z[:, c * LANES:(c + 1) * LANES]
            if rope:
                zc = zc * cos + pltpu.roll(zc, LANES - 32, 1) * sa + pltpu.roll(zc, 32, 1) * sb
            zbuf[c] = zc * mult if mult != 1.0 else zc

    def emit(which):
        for c in range(n_chunks):
            if not by_residue:
                refs[which][:, c * LANES:(c + 1) * LANES] = zbuf[c].astype(refs[which].dtype)
                continue
            g, cl = c // per_group, c % per_group
            ref, d = refs[which * N_GROUPS + g], DILATIONS[g]
            for r in range(d):
                ref[r, :, cl * LANES:(cl + 1) * LANES] = zbuf[c, pl.ds(r, tm // d, stride=d), :].astype(ref.dtype)

    o = 0
    stage(proj(o, o + ATTN_WIDTH), True, ATTN_SCALE)
    emit(0)
    o += ATTN_WIDTH
    stage(proj(o, o + ATTN_WIDTH), True, 1.0)
    emit(1)
    o += ATTN_WIDTH
    stage(proj(o, o + ATTN_WIDTH), False, 1.0)
    emit(2)
    o += ATTN_WIDTH
    bg_ref[...] = _pack(proj(o, o + CONV_WIDTH))
    o += CONV_WIDTH
    cg = proj(o, o + CONV_WIDTH)
    o += CONV_WIDTH
    u_ref[...] = cg * proj(o, o + CONV_WIDTH)
    o += CONV_WIDTH
    ga_ref[...] = _pack(proj(o, o + D_MODEL))
    o += D_MODEL
    gc_ref[...] = _pack(proj(o, o + D_MODEL))


def _inproj(x2d, mod_specs, mods, tab_spec, tabs, n1, w_in_p, residue_seq=None):
    n = x2d.shape[0]
    tm = TM_PROJ
    row = lambda w: pl.BlockSpec((tm, w), lambda i: (i, 0))
    const = lambda s: pl.BlockSpec(s, lambda i: (0, 0), pipeline_mode=pl.Buffered(1))
    if residue_seq is None:
        qkv_shapes = [jax.ShapeDtypeStruct((n, ATTN_WIDTH), dt) for dt in (BF16, F32, F32)]
        qkv_specs = [row(ATTN_WIDTH)] * 3
    else:
        bps = residue_seq // tm
        qkv_shapes, qkv_specs = [], []
        for dt in (BF16, F32, F32):
            for d in DILATIONS:
                qkv_shapes.append(jax.ShapeDtypeStruct((n // residue_seq, d, residue_seq // d, GROUP_WIDTH), dt))
                qkv_specs.append(pl.BlockSpec((None, d, tm // d, GROUP_WIDTH), lambda i: (i // bps, 0, i % bps, 0)))
    half = lambda w: pl.BlockSpec((tm // 2, w), lambda i: (i, 0))
    outs = qkv_shapes + [jax.ShapeDtypeStruct((n // 2, CONV_WIDTH), U32),
                         jax.ShapeDtypeStruct((n, CONV_WIDTH), F32),
                         jax.ShapeDtypeStruct((n // 2, D_MODEL), U32),
                         jax.ShapeDtypeStruct((n // 2, D_MODEL), U32)]
    return pl.pallas_call(
        functools.partial(_inproj_kernel, by_residue=residue_seq is not None),
        out_shape=outs,
        grid=(n // tm,),
        in_specs=[row(D_MODEL), mod_specs, mod_specs, const((1, D_MODEL)),
                  tab_spec, tab_spec, tab_spec, const(w_in_p.shape)],
        out_specs=qkv_specs + [half(CONV_WIDTH), row(CONV_WIDTH), half(D_MODEL), half(D_MODEL)],
        scratch_shapes=[pltpu.VMEM((ATTN_WIDTH // LANES, tm, LANES), F32)],
        compiler_params=_cparams(("arbitrary",)),
        name="inproj",
    )(x2d, mods[0], mods[1], n1, tabs[0], tabs[1], tabs[2], w_in_p)


def _pattn_kernel(q_ref, kp_ref, kc_ref, vp_ref, vc_ref, o_ref, l_ref):
    nb = pl.program_id(2)
    k = jnp.concatenate([kp_ref[...], kc_ref[...]], axis=0).astype(BF16)
    v = jnp.concatenate([vp_ref[...], vc_ref[...]], axis=0).astype(BF16)
    qi = lax.broadcasted_iota(jnp.int32, (ATTN_BLK, 2 * ATTN_BLK), 0)
    ki = lax.broadcasted_iota(jnp.int32, (ATTN_BLK, 2 * ATTN_BLK), 1)
    dist = qi + ATTN_BLK - ki
    band = (dist >= 0) & (dist <= N_BACK)
    band0 = band & (ki >= jnp.where(nb > 0, 0, ATTN_BLK))
    for sb in range(q_ref.shape[0] // ATTN_BLK):
        rows = slice(sb * ATTN_BLK, (sb + 1) * ATTN_BLK)
        keys = slice(sb * ATTN_BLK, (sb + 2) * ATTN_BLK)
        q = q_ref[rows, :]
        valid = band0 if sb == 0 else band
        for h in range(HEADS_PER_GROUP):
            sl = slice(h * HEAD_DIM, (h + 1) * HEAD_DIM)
            s = lax.dot_general(q[:, sl], k[keys, sl], (((1,), (1,)), ((), ())), preferred_element_type=F32)
            s = jnp.where(valid, s, -jnp.inf)
            m = jnp.max(s, axis=-1, keepdims=True)
            p = jnp.exp(s - m)
            den = jnp.sum(p, axis=-1, keepdims=True)
            o = jnp.dot(p.astype(BF16), v[keys, sl], preferred_element_type=F32) / den
            o_ref[rows, sl] = o
            l_ref[rows, sl] = jnp.broadcast_to(m + jnp.log(den), (ATTN_BLK, HEAD_DIM))


def _pattn(q, k, v, g):
    batch, d, n, _ = q.shape
    step = min(ATTN_STEP, n)
    per_blk = step // ATTN_BLK
    cur = pl.BlockSpec((None, None, step, GROUP_WIDTH), lambda b, r, nb: (b, r, nb, 0))
    prev = pl.BlockSpec((None, None, ATTN_BLK, GROUP_WIDTH),
                        lambda b, r, nb: (b, r, jnp.maximum(nb * per_blk - 1, 0), 0))
    return pl.pallas_call(
        _pattn_kernel,
        out_shape=[jax.ShapeDtypeStruct((batch, d, n, GROUP_WIDTH), F32)] * 2,
        grid=(batch, d, n // step),
        in_specs=[cur, prev, cur, prev, cur],
        out_specs=[cur, cur],
        compiler_params=_cparams(("arbitrary",) * 3),
        name=f"pattn{g}",
    )(q, k, k, v, v)


def _sattn_kernel(q_ref, k_ref, v_ref, c0_ref, c1_ref, c2_ref, o_ref, *, s_len):
    for s in range(q_ref.shape[0]):
        _sattn_one(q_ref.at[s], k_ref.at[s], v_ref.at[s], c0_ref.at[s], c1_ref.at[s], c2_ref.at[s], o_ref.at[s],
                   s_len=s_len)


def _sattn_one(q_ref, k_ref, v_ref, c0_ref, c1_ref, c2_ref, o_ref, *, s_len):
    q = q_ref[...].astype(F32)
    kn = k_ref[...]
    vn = v_ref[...]
    rows = HEADS_PER_GROUP * s_len
    lane_head = lax.broadcasted_iota(jnp.int32, (s_len, GROUP_WIDTH), 1) // HEAD_DIM
    pad = jnp.zeros((LANES - s_len, GROUP_WIDTH), F32)
    nt_dims = (((1,), (1,)), ((), ()))
    ms, ls, os_ = [], [], []
    for g, cref in enumerate((c0_ref, c1_ref, c2_ref)):
        d, win = DILATIONS[g], WINDOWS[g]
        gs = slice(g * GROUP_WIDTH, (g + 1) * GROUP_WIDTH)
        qg = q[:, gs]
        qexp = jnp.concatenate(
            [jnp.where(lane_head == h, qg, 0.0) for h in range(HEADS_PER_GROUP)], axis=0).astype(BF16)
        s_c = jnp.dot(qexp, cref[0].astype(BF16), preferred_element_type=F32)
        s_n = lax.dot_general(qexp, jnp.concatenate([kn[:, gs], pad], axis=0).astype(BF16), nt_dims,
                              preferred_element_type=F32)
        back_c = (win + lax.broadcasted_iota(jnp.int32, (rows, win), 0) % s_len
                  - lax.broadcasted_iota(jnp.int32, (rows, win), 1))
        col_n = lax.broadcasted_iota(jnp.int32, (rows, LANES), 1)
        back_n = lax.broadcasted_iota(jnp.int32, (rows, LANES), 0) % s_len - col_n
        s_c = jnp.where((back_c <= win) & ((back_c & (d - 1)) == 0), s_c, -jnp.inf)
        s_n = jnp.where((back_n >= 0) & ((back_n & (d - 1)) == 0) & (col_n < s_len), s_n, -jnp.inf)
        m = jnp.maximum(jnp.max(s_c, axis=-1, keepdims=True), jnp.max(s_n, axis=-1, keepdims=True))
        p_c = jnp.exp(s_c - m)
        p_n = jnp.exp(s_n - m)
        ls.append(jnp.sum(p_c, axis=-1, keepdims=True) + jnp.sum(p_n, axis=-1, keepdims=True))
        ms.append(m)
        o = lax.dot_general(p_c.astype(BF16), cref[1].astype(BF16), nt_dims, preferred_element_type=F32)
        o = o + jnp.dot(p_n.astype(BF16), jnp.concatenate([vn[:, gs], pad], axis=0).astype(BF16),
                        preferred_element_type=F32)
        os_.append(o)
    mm = jnp.maximum(jnp.maximum(ms[0], ms[1]), ms[2])
    num = jnp.zeros((rows, GROUP_WIDTH), F32)
    den = jnp.zeros((rows, 1), F32)
    for g in range(N_GROUPS):
        w = jnp.exp(ms[g] - mm)
        num = num + w * os_[g]
        den = den + w * ls[g]
    full = num / den
    out = jnp.zeros((s_len, GROUP_WIDTH), F32)
    for h in range(HEADS_PER_GROUP):
        out = out + jnp.where(lane_head == h, full[h * s_len:(h + 1) * s_len], 0.0)
    o_ref[...] = out


def _sattn(q, k, v, caches, n_seq, s_len):
    q3 = q.reshape(n_seq, s_len, ATTN_WIDTH)
    k3 = k.reshape(n_seq, s_len, ATTN_WIDTH)
    v3 = v.reshape(n_seq, s_len, ATTN_WIDTH)
    cts = [jnp.transpose(c, (0, 2, 3, 4, 1)).reshape(n_seq, 2, GROUP_WIDTH, c.shape[1]) for c in caches]
    ns = SATTN_SEQS
    tok = pl.BlockSpec((ns, s_len, ATTN_WIDTH), lambda b: (b, 0, 0))
    out = pl.pallas_call(
        functools.partial(_sattn_kernel, s_len=s_len),
        out_shape=jax.ShapeDtypeStruct((n_seq, s_len, GROUP_WIDTH), F32),
        grid=(n_seq // ns,),
        in_specs=[tok, tok, tok] + [pl.BlockSpec((ns, 2, GROUP_WIDTH, w), lambda b: (b, 0, 0, 0)) for w in WINDOWS],
        out_specs=pl.BlockSpec((ns, s_len, GROUP_WIDTH), lambda b: (b, 0, 0)),
        compiler_params=_cparams(("arbitrary",)),
        name="sattn",
    )(q3, k3, v3, *cts)
    return out.reshape(n_seq * s_len, GROUP_WIDTH)


def _merge_tail(o_attn, bg, yc, ga_ref, gc_ref, x_ref, g1_ref, n2_ref, sc2_ref, sh2_ref,
                wa_ref, wc_ref, wo_ref, x1_ref, h2t_ref):
    a_out = jnp.dot(o_attn.astype(BF16), _unpack(wa_ref[...]), preferred_element_type=F32)
    c_out = jnp.dot((bg * yc).astype(BF16), _unpack(wc_ref[...]), preferred_element_type=F32)
    ga, gc = _unpack(ga_ref[...]).astype(F32), _unpack(gc_ref[...]).astype(F32)
    mix = jax.nn.sigmoid(ga) * a_out + jax.nn.sigmoid(gc) * c_out
    mo = jnp.dot(mix.astype(BF16), _unpack(wo_ref[...]), preferred_element_type=F32)
    x1 = x_ref[...] + g1_ref[...] * mo
    x1_ref[...] = x1
    h2 = _rmsnorm_mod(x1, n2_ref[...], sc2_ref[...], sh2_ref[...])
    h2t_ref[...] = pltpu.bitcast(h2.T.astype(BF16), U32)


def _conv3(u, um1, um2, cw_ref):
    return cw_ref[0:1, :] * um2 + cw_ref[1:2, :] * um1 + cw_ref[2:3, :] * u


def _merge_prompt_kernel(o0_ref, o1_ref, o2_ref, l0_ref, l1_ref, l2_ref, bg_ref, u_ref, uh_ref, cw_ref,
                         ga_ref, gc_ref, x_ref, g1_ref, n2_ref, sc2_ref, sh2_ref, wa_ref, wc_ref, wo_ref,
                         h2t_init_ref, x1_ref, h2t_ref, ubuf, *rowbufs, blocks_per_seq):
    del h2t_init_ref
    tm = u_ref.shape[0]
    first = (pl.program_id(0) % blocks_per_seq) == 0

    def by_position(ref, buf):
        d, rows = ref.shape[0], ref.shape[1]
        if d == 1:
            return ref[0]
        for r in range(d):
            for c in range(GROUP_WIDTH // LANES):
                buf[c, pl.ds(r, rows, stride=d), :] = ref[r, :, c * LANES:(c + 1) * LANES]
        return jnp.concatenate([buf[c] for c in range(GROUP_WIDTH // LANES)], axis=1)

    o0, l0 = by_position(o0_ref, None), by_position(l0_ref, None)
    o1, l1 = by_position(o1_ref, rowbufs[0]), by_position(l1_ref, rowbufs[1])
    o2, l2 = by_position(o2_ref, rowbufs[2]), by_position(l2_ref, rowbufs[3])
    mm = jnp.maximum(jnp.maximum(l0, l1), l2)
    e0, e1, e2 = jnp.exp(l0 - mm), jnp.exp(l1 - mm), jnp.exp(l2 - mm)
    o_attn = (e0 * o0 + e1 * o1 + e2 * o2) / (e0 + e1 + e2)
    u = u_ref[...]
    ubuf[0:SUBLANES, :] = jnp.where(first, 0.0, uh_ref[...])
    ubuf[SUBLANES:SUBLANES + tm, :] = u
    yc = _conv3(u, ubuf[SUBLANES - 1:SUBLANES - 1 + tm, :], ubuf[SUBLANES - 2:SUBLANES - 2 + tm, :], cw_ref)
    _merge_tail(o_attn, _unpack(bg_ref[...]).astype(F32), yc, ga_ref, gc_ref, x_ref, g1_ref, n2_ref, sc2_ref, sh2_ref,
                wa_ref, wc_ref, wo_ref, x1_ref, h2t_ref)


def _merge_sample_kernel(oa_ref, bg_ref, u_ref, p1_ref, p2_ref, cw_ref,
                         ga_ref, gc_ref, x_ref, g1_ref, n2_ref, sc2_ref, sh2_ref, wa_ref, wc_ref, wo_ref,
                         h2t_all_ref, x1_ref, h2t_ref, ubuf, *, s_len):
    del h2t_all_ref
    tm = u_ref.shape[0]
    u = u_ref[...]
    ubuf[0:SUBLANES, :] = jnp.zeros((SUBLANES, CONV_WIDTH), F32)
    ubuf[SUBLANES:SUBLANES + tm, :] = u
    t = lax.broadcasted_iota(jnp.int32, (tm, CONV_WIDTH), 0) % s_len
    um1 = jnp.where(t < 1, p1_ref[...], ubuf[SUBLANES - 1:SUBLANES - 1 + tm, :])
    um2 = jnp.where(t < 2, p2_ref[...], ubuf[SUBLANES - 2:SUBLANES - 2 + tm, :])
    yc = _conv3(u, um1, um2, cw_ref)
    _merge_tail(oa_ref[...], _unpack(bg_ref[...]).astype(F32), yc, ga_ref, gc_ref, x_ref, g1_ref, n2_ref, sc2_ref, sh2_ref,
                wa_ref, wc_ref, wo_ref, x1_ref, h2t_ref)


def _merge_common_specs(tm, mod_spec):
    row = lambda w: pl.BlockSpec((tm, w), lambda i: (i, 0))
    const = lambda s: pl.BlockSpec(s, lambda i: (0, 0), pipeline_mode=pl.Buffered(1))
    half = lambda w: pl.BlockSpec((tm // 2, w), lambda i: (i, 0))
    ins = [half(D_MODEL), half(D_MODEL), row(D_MODEL), mod_spec, const((1, D_MODEL)), mod_spec, mod_spec,
           const((GROUP_WIDTH // 2, D_MODEL)), const((CONV_WIDTH // 2, D_MODEL)), const((D_MODEL // 2, D_MODEL))]
    outs = [row(D_MODEL), pl.BlockSpec((D_MODEL // 2, tm), lambda i: (0, i))]
    return ins, outs


def _merge_prompt(attn, bg, u, conv_w, ga, gc, x2d, mod_spec, g1, n2, sc2, sh2, wa, wc, wo, seq, n_total):
    n = x2d.shape[0]
    tm = TM_MERGE
    row = lambda w: pl.BlockSpec((tm, w), lambda i: (i, 0))
    halo = pl.BlockSpec((SUBLANES, CONV_WIDTH), lambda i: (jnp.maximum(i * (tm // SUBLANES) - 1, 0), 0))
    ins, outs = _merge_common_specs(tm, mod_spec)
    (o0, l0), (o1, l1), (o2, l2) = attn
    bps = seq // tm
    grp = [pl.BlockSpec((None, d, tm // d, GROUP_WIDTH), lambda i: (i // bps, 0, i % bps, 0)) for d in DILATIONS]
    return pl.pallas_call(
        functools.partial(_merge_prompt_kernel, blocks_per_seq=bps),
        out_shape=[jax.ShapeDtypeStruct((n, D_MODEL), F32), jax.ShapeDtypeStruct((D_MODEL // 2, n_total), U32)],
        grid=(n // tm,),
        in_specs=grp + grp + [pl.BlockSpec((tm // 2, CONV_WIDTH), lambda i: (i, 0)), row(CONV_WIDTH), halo,
                              pl.BlockSpec((CONV_K, CONV_WIDTH), lambda i: (0, 0))] + ins
                 + [pl.BlockSpec(memory_space=pl.ANY)],
        out_specs=outs,
        input_output_aliases={len(grp) * 2 + 4 + len(ins): 1},
        scratch_shapes=[pltpu.VMEM((SUBLANES + tm, CONV_WIDTH), F32)] + [pltpu.VMEM((GROUP_WIDTH // LANES, tm, LANES), F32)] * 4,
        compiler_params=_cparams(("arbitrary",)),
        name="merge_prompt",
    )(o0, o1, o2, l0, l1, l2, bg, u, u, conv_w, ga, gc, x2d, g1, n2, sc2, sh2, wa, wc, wo,
      jnp.zeros((D_MODEL // 2, n_total), U32))


def _merge_sample(o_attn, bg, u, p1, p2, conv_w, ga, gc, x2d, mod_spec, g1, n2, sc2, sh2, wa, wc, wo, s_len,
                  h2t_all, col_off):
    n = x2d.shape[0]
    tm = TM_MERGE
    row = lambda w: pl.BlockSpec((tm, w), lambda i: (i, 0))
    ins, outs = _merge_common_specs(tm, mod_spec)
    outs = [outs[0], pl.BlockSpec((D_MODEL // 2, tm), lambda i: (0, i + col_off // tm))]
    in_specs = ([row(GROUP_WIDTH), pl.BlockSpec((tm // 2, CONV_WIDTH), lambda i: (i, 0))] + [row(CONV_WIDTH)] * 3
                + [pl.BlockSpec((CONV_K, CONV_WIDTH), lambda i: (0, 0))] + ins + [pl.BlockSpec(memory_space=pl.ANY)])
    return pl.pallas_call(
        functools.partial(_merge_sample_kernel, s_len=s_len),
        out_shape=[jax.ShapeDtypeStruct((n, D_MODEL), F32), jax.ShapeDtypeStruct(h2t_all.shape, U32)],
        grid=(n // tm,),
        in_specs=in_specs,
        out_specs=outs,
        input_output_aliases={len(in_specs) - 1: 1},
        scratch_shapes=[pltpu.VMEM((SUBLANES + tm, CONV_WIDTH), F32)],
        compiler_params=_cparams(("arbitrary",)),
        name="merge_sample",
    )(o_attn, bg, u, p1, p2, conv_w, ga, gc, x2d, g1, n2, sc2, sh2, wa, wc, wo, h2t_all)


def _oddeven_merge_sort_pairs(n):
    pairs = []
    p = 1
    while p < n:
        k = p
        while k >= 1:
            for j in range(k % p, n - k, 2 * k):
                for i in range(min(k, n - j - k)):
                    if (i + j) // (2 * p) == (i + j + k) // (2 * p):
                        pairs.append((i + j, i + j + k))
            k //= 2
        p *= 2
    return pairs


_SORT16 = _oddeven_merge_sort_pairs(PEER_TOPK)


def _cmpx(x, a, b):
    hi, lo = jnp.maximum(x[a], x[b]), jnp.minimum(x[a], x[b])
    x[a], x[b] = hi, lo


def _bitonic_clean(x):
    stride = PEER_TOPK // 2
    while stride >= 1:
        for i in range(PEER_TOPK):
            if i & stride == 0:
                _cmpx(x, i, i + stride)
        stride //= 2
    return x


def _merge_top16(a, b):
    c = []
    for j in range(PEER_TOPK):
        jb = PEER_TOPK - 1 - j
        c.append(jnp.maximum(a[j], b[jb]) if jb < len(b) else a[j])
    return _bitonic_clean(c)


def _top16_over_keys(s):
    x = [s[j * SUBLANES:(j + 1) * SUBLANES, :] for j in range(N_KEYS // SUBLANES)]
    for a, b in _SORT16:
        _cmpx(x, a, b)
    shift = SUBLANES // 2
    while shift >= 1:
        x = _merge_top16(x, [pltpu.roll(v, shift, 0) for v in x])
        shift //= 2
    return x


def _route_kernel(h_ref, wq_ref, k1_ref, k2_ref, cnt_ref, m1_ref, r2_ref, e2_ref):
    tk = h_ref.shape[1]
    qt = jnp.dot(_unpack(wq_ref[...]), _unpack(h_ref[...]), preferred_element_type=F32).astype(BF16)
    k1, k2 = k1_ref[...], k2_ref[...]
    sub = lax.broadcasted_iota(jnp.int32, (SUBLANES, tk), 0)
    half = N_KEYS
    s1s, s2s = [], []
    v1 = [jnp.zeros((SUBLANES, tk), F32) for _ in range(PEER_TOPK)]
    v2 = [jnp.zeros((SUBLANES, tk), F32) for _ in range(PEER_TOPK)]
    for h in range(PEER_HEADS):
        base = h * 2 * half
        s1 = jnp.dot(k1, qt[base:base + half], preferred_element_type=F32)
        s2 = jnp.dot(k2, qt[base + half:base + 2 * half], preferred_element_type=F32)
        s1s.append(s1)
        s2s.append(s2)
        t1 = _top16_over_keys(s1)
        t2 = _top16_over_keys(s2)
        for j in range(PEER_TOPK):
            v1[j] = jnp.where(sub == h, t1[j], v1[j])
            v2[j] = jnp.where(sub == h, t2[j], v2[j])
    psum = {}
    for a in range(PEER_TOPK):
        for b in range(PEER_TOPK // (a + 1)):
            psum[a, b] = v1[a] + v2[b]
    lists = [[psum[a, b] for b in range(PEER_TOPK // (a + 1))] for a in range(PEER_TOPK // 2)]
    lists.append([psum[a, 0] for a in range(PEER_TOPK // 2, PEER_TOPK)])
    top = lists[0]
    for other in lists[1:]:
        top = _merge_top16(top, other)
    tau = top[PEER_TOPK - 1]
    z = jnp.ones((SUBLANES, tk), F32)
    for j in range(1, PEER_TOPK):
        z = z + jnp.exp(top[j] - top[0])
    rz = 1.0 / z
    x1 = []
    for b in range(PEER_TOPK):
        x = jnp.full((SUBLANES, tk), jnp.inf, F32)
        for a in range(PEER_TOPK // (b + 1)):
            x = jnp.where(psum[a, b] >= tau, v1[a], x)
        x1.append(x)
    for h in range(PEER_HEADS):
        s1, s2 = s1s[h], s2s[h]
        row = lambda v: v[h:h + 1, :]
        cnt = jnp.zeros_like(s1)
        rank = jnp.full_like(s2, float(PEER_TOPK))
        for j in range(PEER_TOPK):
            cnt = jnp.where(s1 >= row(x1[j]), float(j + 1), cnt)
        for j in range(PEER_TOPK - 1, -1, -1):
            rank = jnp.where(s2 >= row(v2[j]), float(j), rank)
        m1 = jnp.exp(s1 - row(v1[0])) * row(rz)
        cnt_ref[:, h, :, :] = cnt.reshape(N_KEYS // SUBLANES, SUBLANES, tk)
        m1_ref[:, h, :, :] = m1.reshape(N_KEYS // SUBLANES, SUBLANES, tk)
        half_rows = slice(h * N_KEYS // 2, (h + 1) * N_KEYS // 2)
        r2_ref[half_rows, :] = pltpu.bitcast(rank.astype(BF16), U32)
        e2_ref[half_rows, :] = pltpu.bitcast(jnp.exp(s2 - row(v2[0])).astype(BF16), U32)


def _route(h2t, wq_t, k1, k2):
    n = h2t.shape[1]
    tk = TK_ROUTE
    a8 = N_KEYS // SUBLANES
    s14 = jax.ShapeDtypeStruct((a8, PEER_HEADS, SUBLANES, n), F32)
    s2d = jax.ShapeDtypeStruct((PEER_HEADS * N_KEYS // 2, n), U32)
    spec4 = pl.BlockSpec((a8, PEER_HEADS, SUBLANES, tk), lambda i: (0, 0, 0, i))
    spec2 = pl.BlockSpec((PEER_HEADS * N_KEYS // 2, tk), lambda i: (0, i))
    return pl.pallas_call(
        _route_kernel,
        out_shape=[s14, s14, s2d, s2d],
        grid=(n // tk,),
        in_specs=[pl.BlockSpec((D_MODEL // 2, tk), lambda i: (0, i)),
                  pl.BlockSpec(wq_t.shape, lambda i: (0, 0)),
                  pl.BlockSpec(k1.shape, lambda i: (0, 0)),
                  pl.BlockSpec(k2.shape, lambda i: (0, 0))],
        out_specs=[spec4, spec4, spec2, spec2],
        compiler_params=_cparams(("arbitrary",)),
        name="peer_route",
    )(h2t, wq_t, k1, k2)


FFN_MXU_TILE = 256
FFN_VPU_TILE = LANES
BF16_ROWS = 16
FFN_ROW_CHUNK = 64


def _ffn_kernel(h_ref, u_ref, vt_ref, cnt_ref, m1_ref, r2_ref, e2_ref, o_ref, coef_ref, *act_refs):
    tm = h_ref.shape[1]
    n_bt = N_KEYS // BF16_ROWS

    def row16(ref, h, al, ls):
        row = ref[al // SUBLANES, h, al % SUBLANES:al % SUBLANES + 1, ls]
        return jnp.broadcast_to(row, (BF16_ROWS, FFN_VPU_TILE)).astype(BF16)

    def mxu_cols(mt):
        return slice(mt * FFN_MXU_TILE, (mt + 1) * FFN_MXU_TILE)

    def build_gates(mt):
        for vt in range(FFN_MXU_TILE // FFN_VPU_TILE):
            lo = mt * FFN_MXU_TILE + vt * FFN_VPU_TILE
            ls = slice(lo, lo + FFN_VPU_TILE)
            for a0 in range(0, EB_FFN // N_KEYS, 2):
                acc = [[jnp.zeros((BF16_ROWS, FFN_VPU_TILE), BF16) for _ in range(n_bt)] for _ in range(2)]
                for h in range(PEER_HEADS):
                    cnts = [row16(cnt_ref, h, a0 + k, ls) for k in range(2)]
                    m1s = [row16(m1_ref, h, a0 + k, ls) for k in range(2)]
                    for bt in range(n_bt):
                        rs = slice((h * N_KEYS + bt * BF16_ROWS) // 2, (h * N_KEYS + (bt + 1) * BF16_ROWS) // 2)
                        rank, e2 = _unpack(r2_ref[rs, ls]), _unpack(e2_ref[rs, ls])
                        for k in range(2):
                            acc[k][bt] = acc[k][bt] + jnp.where(rank < cnts[k], e2, jnp.zeros_like(e2)) * m1s[k]
                for k in range(2):
                    for bt in range(n_bt):
                        es = slice((a0 + k) * N_KEYS + bt * BF16_ROWS, (a0 + k) * N_KEYS + (bt + 1) * BF16_ROWS)
                        coef_ref[es, ls] = acc[k][bt]

    def up_proj(mt):
        act_refs[mt % 2][...] = jnp.dot(_unpack(u_ref[...]), _unpack(h_ref[:, mxu_cols(mt)]),
                                        preferred_element_type=F32)

    def down_proj(mt):
        ms = mxu_cols(mt)
        k0 = (2.0 / jnp.pi) ** 0.5
        zero = jnp.minimum(pl.program_id(1), 0)
        for rc in range(EB_FFN // FFN_ROW_CHUNK):
            rs = slice(rc * FFN_ROW_CHUNK, (rc + 1) * FFN_ROW_CHUNK)
            x = act_refs[mt % 2][pl.ds(pl.multiple_of(zero + rc * FFN_ROW_CHUNK, FFN_ROW_CHUNK), FFN_ROW_CHUNK), :]
            c0 = -2.0 * k0 * LOG2_E
            gelu = x / (1.0 + jnp.exp2(x * (c0 + (c0 * 0.044715) * (x * x))))
            coef_ref[rs, ms] = coef_ref[rs, ms] * gelu.astype(BF16)
        o_ref[:, ms] += jnp.dot(_unpack(vt_ref[...]), coef_ref[:, ms], preferred_element_type=F32)

    @pl.when(pl.program_id(1) == 0)
    def _():
        o_ref[...] = jnp.zeros_like(o_ref)

    n_mt = tm // FFN_MXU_TILE
    build_gates(0)
    up_proj(0)
    for mt in range(n_mt):
        if mt + 1 < n_mt:
            build_gates(mt + 1)
            up_proj(mt + 1)
        down_proj(mt)


def _ffn(h2t, u_b, vt_b, cnt, m1, r2, e2):
    n = h2t.shape[1]
    tm, eb = TM_FFN, EB_FFN
    n_exp = 2 * u_b.shape[0]
    spec4 = pl.BlockSpec((eb // (N_KEYS * SUBLANES), PEER_HEADS, SUBLANES, tm), lambda i, e: (e, 0, 0, i))
    spec2 = pl.BlockSpec((PEER_HEADS * N_KEYS // 2, tm), lambda i, e: (0, i))
    return pl.pallas_call(
        _ffn_kernel,
        out_shape=jax.ShapeDtypeStruct((D_MODEL, n), F32),
        grid=(n // tm, n_exp // eb),
        in_specs=[pl.BlockSpec((D_MODEL // 2, tm), lambda i, e: (0, i)),
                  pl.BlockSpec((eb // 2, D_MODEL), lambda i, e: (e, 0)),
                  pl.BlockSpec((D_MODEL // 2, eb), lambda i, e: (0, e)),
                  spec4, spec4, spec2, spec2],
        out_specs=pl.BlockSpec((D_MODEL, tm), lambda i, e: (0, i)),
        scratch_shapes=[pltpu.VMEM((eb, tm), BF16)] + [pltpu.VMEM((eb, FFN_MXU_TILE), F32)] * 2,
        compiler_params=_cparams(("arbitrary", "arbitrary")),
        name="peer_ffn",
    )(h2t, u_b, vt_b, cnt, m1, r2, e2)


def _final_kernel(p_ref, x_ref, g2_ref, nf_ref, y_ref):
    x2 = x_ref[...] + g2_ref[...] * p_ref[...].T
    ms = jnp.mean(x2 * x2, axis=-1, keepdims=True)
    y_ref[...] = x2 * lax.rsqrt(ms + NORM_EPS) * nf_ref[...]


def _final(peer_t, col_off, x1, mod_spec, g2, nf):
    n = x1.shape[0]
    tm = TM_FINAL
    off = col_off // tm
    return pl.pallas_call(
        _final_kernel,
        out_shape=jax.ShapeDtypeStruct((n, D_MODEL), F32),
        grid=(n // tm,),
        in_specs=[pl.BlockSpec((D_MODEL, tm), lambda i: (0, i + off)),
                  pl.BlockSpec((tm, D_MODEL), lambda i: (i, 0)),
                  mod_spec,
                  pl.BlockSpec((1, D_MODEL), lambda i: (0, 0))],
        out_specs=pl.BlockSpec((tm, D_MODEL), lambda i: (i, 0)),
        compiler_params=_cparams(("arbitrary",)),
        name="final",
    )(peer_t, x1, g2, nf)


def _rope_tables(pos):
    half = HEAD_DIM // 2
    inv = ROPE_THETA ** (-jnp.arange(half, dtype=F32) / half)
    ang = pos.astype(F32)[:, None] * inv[None, :]
    cos, sin = jnp.cos(ang), jnp.sin(ang)
    zero = jnp.zeros_like(sin)
    reps = LANES // HEAD_DIM
    cos_t = jnp.tile(jnp.concatenate([cos, cos], axis=1), (1, reps))
    sa_t = jnp.tile(jnp.concatenate([-sin, zero], axis=1), (1, reps))
    sb_t = jnp.tile(jnp.concatenate([zero, sin], axis=1), (1, reps))
    return cos_t, sa_t, sb_t


def kernel(x_prompt, x_sample, cache_kv0, cache_kv1, cache_kv2, state_conv, c_prompt, c_sample,
           norm1_g, norm2_g, norm_f_g, w_ada, b_ada, w_in, conv_w, w_attn_o, w_conv_o, w_o,
           w_query, sub_keys, expert_u, expert_v):
    batch, seq, _ = x_prompt.shape
    n_seq, s_len, _ = x_sample.shape
    depth = w_in.shape[0]
    assert depth == 1
    assert tuple(c.shape[2] for c in (cache_kv0, cache_kv1, cache_kv2)) == WINDOWS
    assert s_len == SUBLANES and seq % (ATTN_BLK * DILATIONS[2]) == 0 and n_seq % SATTN_SEQS == 0
    n_p, n_s = batch * seq, n_seq * s_len
    l = 0

    w_in_b = _pack_weight(w_in[l])
    wa_b, wc_b, wo_b = _pack_weight(w_attn_o[l]), _pack_weight(w_conv_o[l]), _pack_weight(w_o[l])
    wq_t = _pack_weight(w_query[l], transpose=True)
    k1_b, k2_b = sub_keys[l, 0].astype(BF16), sub_keys[l, 1].astype(BF16)
    u_b = _pack_weight(expert_u[l])
    vt_b = _pack_weight(expert_v[l], transpose=True)
    n1, n2, nf = norm1_g[l][None, :], norm2_g[l][None, :], norm_f_g[None, :]

    n_c = batch + n_seq
    c_all = jnp.concatenate([c_prompt, c_sample, jnp.zeros((-n_c % SUBLANES, D_MODEL), F32)], axis=0)
    mod = _ada(c_all, w_ada[l], b_ada[l])
    mod_p = [m[:, None, :] for m in jnp.split(mod[:batch], 6, axis=-1)]
    mod_s = jnp.split(jnp.repeat(mod[batch:n_c], s_len, axis=0), 6, axis=-1)

    def seq_spec(tm):
        return pl.BlockSpec((None, 1, D_MODEL), lambda i: (i // (seq // tm), 0, 0))

    def tok_spec(tm):
        return pl.BlockSpec((tm, D_MODEL), lambda i: (i, 0))

    tabs_p = _rope_tables(jnp.arange(seq))
    tabs_s = [jnp.tile(t, (n_seq, 1)) for t in _rope_tables(PAST_LEN + jnp.arange(s_len))]
    tab_p_spec = pl.BlockSpec((TM_PROJ, LANES), lambda i: (i % (seq // TM_PROJ), 0))
    tab_s_spec = pl.BlockSpec((TM_PROJ, LANES), lambda i: (i, 0))

    xp = x_prompt.reshape(n_p, D_MODEL)
    xs = x_sample.reshape(n_s, D_MODEL)
    *qkv_p, bgp, up, gap, gcp = _inproj(xp, seq_spec(TM_PROJ), (mod_p[1], mod_p[0]), tab_p_spec, tabs_p, n1, w_in_b,
                                        residue_seq=seq)
    qp, kp, vp = qkv_p[0:N_GROUPS], qkv_p[N_GROUPS:2 * N_GROUPS], qkv_p[2 * N_GROUPS:]
    qs, ks, vs, bgs, us, gas, gcs = _inproj(xs, tok_spec(TM_PROJ), (mod_s[1], mod_s[0]), tab_s_spec, tabs_s, n1, w_in_b)

    attn_p = [_pattn(qp[g], kp[g], vp[g], g) for g in range(N_GROUPS)]
    caches = (cache_kv0[l], cache_kv1[l], cache_kv2[l])
    attn_s = _sattn(qs, ks, vs, caches, n_seq, s_len)

    x1p, h2tp = _merge_prompt(attn_p, bgp, up, conv_w[l], gap, gcp, xp, seq_spec(TM_MERGE),
                              mod_p[2], n2, mod_p[4], mod_p[3], wa_b, wc_b, wo_b, seq, n_p + n_s)
    st = state_conv[l]
    p1 = jnp.pad(st[:, 1:2], ((0, 0), (0, s_len - 1), (0, 0))).reshape(n_s, CONV_WIDTH)
    p2 = jnp.pad(st, ((0, 0), (0, s_len - 2), (0, 0))).reshape(n_s, CONV_WIDTH)
    x1s, h2t = _merge_sample(attn_s, bgs, us, p1, p2, conv_w[l], gas, gcs, xs, tok_spec(TM_MERGE),
                             mod_s[2], n2, mod_s[4], mod_s[3], wa_b, wc_b, wo_b, s_len, h2tp, n_p)
    cnt, m1, r2, e2 = _route(h2t, wq_t, k1_b, k2_b)
    peer_t = _ffn(h2t, u_b, vt_b, cnt, m1, r2, e2)

    yp = _final(peer_t, 0, x1p, seq_spec(TM_FINAL), mod_p[5], nf).reshape(batch, seq, D_MODEL)
    ys = _final(peer_t, n_p, x1s, tok_spec(TM_FINAL), mod_s[5], nf).reshape(n_seq, s_len, D_MODEL)

    def prompt_tail(a, g):
        d, n_rows = a.shape[1], a.shape[2]
        t = jnp.swapaxes(a[:, :, n_rows - N_BACK:, :], 1, 2)
        return t.reshape(batch, N_BACK * d, HEADS_PER_GROUP, HEAD_DIM)

    kv_p, kv_s = [], []
    for g in range(N_GROUPS):
        assert WINDOWS[g] == N_BACK * DILATIONS[g] <= seq
        gs = slice(g * GROUP_WIDTH, (g + 1) * GROUP_WIDTH)
        kv_p.append(jnp.stack([prompt_tail(kp[g], g), prompt_tail(vp[g], g)], axis=2)[None])
        k4 = ks.reshape(n_seq, s_len, ATTN_WIDTH)[:, :, gs].reshape(n_seq, s_len, HEADS_PER_GROUP, HEAD_DIM)
        v4 = vs.reshape(n_seq, s_len, ATTN_WIDTH)[:, :, gs].reshape(n_seq, s_len, HEADS_PER_GROUP, HEAD_DIM)
        kv_s.append(jnp.stack([k4, v4], axis=2)[None])
    conv_p = up.reshape(batch, seq, CONV_WIDTH)[:, seq - (CONV_K - 1):][None]
    conv_s = jnp.concatenate([st, us.reshape(n_seq, s_len, CONV_WIDTH)], axis=1)[:, s_len:][None]
    return (yp, ys, kv_p[0], kv_p[1], kv_p[2], conv_p, kv_s[0], kv_s[1], kv_s[2], conv_s)
```
